```python
import jax
import jax.numpy as jnp
from jax import lax
import numpy as np

D_MODEL = 2048
BATCH = 2
SEQ = 8192
DEPTH = 4

GRID_W = 64
N_MIXERS = 2
N_NA_LAYERS = (DEPTH + N_MIXERS - 1) // N_MIXERS
N_MLA_LAYERS = DEPTH // N_MIXERS
RMS_EPS = 1e-6

NA_HEADS = 16
NA_HEAD_DIM = 128
NA_WIDTH = NA_HEADS * NA_HEAD_DIM
NA_KH_MAX = 8
NA_KW = 16

MLA_HEADS = 16
Q_LORA_RANK = 512
KV_LORA_RANK = 512
QK_NOPE_DIM = 128
QK_ROPE_DIM = 64
QK_HEAD_DIM = QK_NOPE_DIM + QK_ROPE_DIM
V_HEAD_DIM = 128
MLA_WIDTH = MLA_HEADS * V_HEAD_DIM
MLA_IN_DIM = Q_LORA_RANK + KV_LORA_RANK + QK_ROPE_DIM + MLA_WIDTH
ROPE_THETA = 10000.0
Q_BLOCK = 128

kernel_name = "hybrid_na_mla_sandwich_encoder"


def rms_norm(x, g):
    xf = x.astype(jnp.float32)
    y = xf * lax.rsqrt(jnp.mean(xf * xf, axis=-1, keepdims=True) + RMS_EPS)
    return (y * g.astype(jnp.float32)).astype(x.dtype)


def rope_tables(length):
    inv_freq = 1.0 / (ROPE_THETA ** (jnp.arange(0, QK_ROPE_DIM, 2, dtype=jnp.float32) / QK_ROPE_DIM))
    ang = jnp.arange(length, dtype=jnp.float32)[:, None] * inv_freq[None, :]
    return jnp.cos(ang), jnp.sin(ang)


def apply_rope(x, cos, sin):
    xf = x.astype(jnp.float32)
    x1, x2 = xf[..., 0::2], xf[..., 1::2]
    c, s = cos[None, :, None, :], sin[None, :, None, :]
    out = jnp.stack([x1 * c - x2 * s, x1 * s + x2 * c], axis=-1)
    return out.reshape(x.shape).astype(x.dtype)


def neighbourhood_attention(q, k, v, rpb):
    B, L, H, Dh = q.shape
    rows = L // GRID_W
    kh = min(NA_KH_MAX, rows)
    kw = NA_KW
    qg = q.reshape(B, rows, GRID_W, H, Dh)
    kg = k.reshape(B, rows, GRID_W, H, Dh)
    vg = v.reshape(B, rows, GRID_W, H, Dh)
    col = jnp.arange(GRID_W)
    col_start = jnp.clip(col - kw // 2, 0, GRID_W - kw)
    col_idx = col_start[:, None] + jnp.arange(kw)[None, :]
    dc = col_idx - col[:, None] + (kw - 1)
    scale = Dh ** -0.5

    def one_row(r):
        row_start = jnp.clip(r - kh // 2, 0, rows - kh)
        k_rows = lax.dynamic_slice_in_dim(kg, row_start, kh, axis=1)
        v_rows = lax.dynamic_slice_in_dim(vg, row_start, kh, axis=1)
        k_nb = jnp.take(k_rows, col_idx, axis=2)
        v_nb = jnp.take(v_rows, col_idx, axis=2)
        q_r = lax.dynamic_index_in_dim(qg, r, axis=1, keepdims=False)
        s = jnp.einsum('bqhd,bxqyhd->bhqxy', q_r, k_nb).astype(jnp.float32) * scale
        dr = row_start + jnp.arange(kh) - r + (NA_KH_MAX - 1)
        bias = rpb[:, dr[None, :, None], dc[:, None, :]]
        s = s + bias[None].astype(jnp.float32)
        p = jax.nn.softmax(s.reshape(B, H, GRID_W, kh * kw), axis=-1)
        p = p.reshape(B, H, GRID_W, kh, kw).astype(v.dtype)
        return jnp.einsum('bhqxy,bxqyhd->bqhd', p, v_nb)

    out = lax.map(one_row, jnp.arange(rows))
    return jnp.transpose(out, (1, 0, 2, 3, 4)).reshape(B, L, H * Dh)


def dense_attention(q, k, v, scale):
    B, L, H, Dq = q.shape
    Dv = v.shape[-1]
    nb = L // Q_BLOCK
    qb = jnp.transpose(q.reshape(B, nb, Q_BLOCK, H, Dq), (1, 0, 2, 3, 4))

    def one_block(qi):
        s = jnp.einsum('bqhd,bkhd->bhqk', qi, k).astype(jnp.float32) * scale
        p = jax.nn.softmax(s, axis=-1).astype(v.dtype)
        return jnp.einsum('bhqk,bkhd->bqhd', p, v)

    o = lax.map(one_block, qb)
    return jnp.transpose(o, (1, 0, 2, 3, 4)).reshape(B, L, H * Dv)


def na_mixer(h, w_in, rpb):
    B, L, _ = h.shape
    q, k, v, z = jnp.split(h @ w_in, 4, axis=-1)
    shp = (B, L, NA_HEADS, NA_HEAD_DIM)
    o = neighbourhood_attention(q.reshape(shp), k.reshape(shp), v.reshape(shp), rpb)
    return o, z


def mla_mixer(h, w_in, q_norm, w_q_b, kv_norm, w_kv_b):
    B, L, _ = h.shape
    splits = [Q_LORA_RANK, Q_LORA_RANK + KV_LORA_RANK, Q_LORA_RANK + KV_LORA_RANK + QK_ROPE_DIM]
    c_q, c_kv, k_rope, z = jnp.split(h @ w_in, splits, axis=-1)
    q = (rms_norm(c_q, q_norm) @ w_q_b).reshape(B, L, MLA_HEADS, QK_HEAD_DIM)
    kv = (rms_norm(c_kv, kv_norm) @ w_kv_b).reshape(B, L, MLA_HEADS, QK_NOPE_DIM + V_HEAD_DIM)
    cos, sin = rope_tables(L)
    q = jnp.concatenate([q[..., :QK_NOPE_DIM], apply_rope(q[..., QK_NOPE_DIM:], cos, sin)], axis=-1)
    k_r = apply_rope(k_rope[:, :, None, :], cos, sin)
    k = jnp.concatenate([kv[..., :QK_NOPE_DIM],
                         jnp.broadcast_to(k_r, (B, L, MLA_HEADS, QK_ROPE_DIM))], axis=-1)
    v = kv[..., QK_NOPE_DIM:]
    o = dense_attention(q, k, v, QK_HEAD_DIM ** -0.5)
    return o, z


def setup_inputs(seed: int = 0) -> dict:
    key = jax.random.key(seed)
    ks = jax.random.split(key, 13)

    def w(k, shape, fan_in):
        return jax.random.normal(k, shape, jnp.float32) * fan_in ** -0.5

    def gain(k, shape):
        return 1.0 + 0.05 * jax.random.normal(k, shape, jnp.float32)

    return {
        "x": jax.random.normal(ks[0], (BATCH, SEQ, D_MODEL), jnp.float32),
        "norm_pre": gain(ks[1], (DEPTH, D_MODEL)),
        "norm_post": gain(ks[2], (DEPTH, D_MODEL)),
        "na_w_in": w(ks[3], (N_NA_LAYERS, D_MODEL, 4 * NA_WIDTH), D_MODEL),
        "na_rpb": 0.1 * jax.random.normal(ks[4], (N_NA_LAYERS, NA_HEADS, 2 * NA_KH_MAX - 1, 2 * NA_KW - 1), jnp.float32),
        "na_w_out": w(ks[5], (N_NA_LAYERS, NA_WIDTH, D_MODEL), NA_WIDTH),
        "mla_w_in": w(ks[6], (N_MLA_LAYERS, D_MODEL, MLA_IN_DIM), D_MODEL),
        "mla_q_norm": gain(ks[7], (N_MLA_LAYERS, Q_LORA_RANK)),
        "mla_w_q_b": w(ks[8], (N_MLA_LAYERS, Q_LORA_RANK, MLA_HEADS * QK_HEAD_DIM), Q_LORA_RANK),
        "mla_kv_norm": gain(ks[9], (N_MLA_LAYERS, KV_LORA_RANK)),
        "mla_w_kv_b": w(ks[10], (N_MLA_LAYERS, KV_LORA_RANK, MLA_HEADS * (QK_NOPE_DIM + V_HEAD_DIM)), KV_LORA_RANK),
        "mla_w_out": w(ks[11], (N_MLA_LAYERS, MLA_WIDTH, D_MODEL), MLA_WIDTH),
    }


def reference(x, norm_pre, norm_post, na_w_in, na_rpb, na_w_out, mla_w_in, mla_q_norm,
              mla_w_q_b, mla_kv_norm, mla_w_kv_b, mla_w_out):
    for i in range(DEPTH):
        h = rms_norm(x, norm_pre[i])
        j = i // N_MIXERS
        if i % N_MIXERS == 0:
            o, z = na_mixer(h, na_w_in[j], na_rpb[j])
            w_out = na_w_out[j]
        else:
            o, z = mla_mixer(h, mla_w_in[j], mla_q_norm[j], mla_w_q_b[j], mla_kv_norm[j], mla_w_kv_b[j])
            w_out = mla_w_out[j]
        y = (o * jax.nn.silu(z)) @ w_out
        x = x + rms_norm(y, norm_post[i])
    return x
```

```python
import functools
import math

import jax
import jax.numpy as jnp
from jax import lax
from jax.experimental import pallas as pl
from jax.experimental.pallas import tpu as pltpu

F32 = jnp.float32
BF16 = jnp.bfloat16

D_MODEL = 2048
DEPTH = 4
GRID_W = 64
RMS_EPS = 1e-6
LOG2E = math.log2(math.e)
NEG = -1e30

NA_HEADS = 16
NA_HEAD_DIM = 128
NA_WIDTH = NA_HEADS * NA_HEAD_DIM
NA_KH = 8
NA_KW = 16
NA_BIAS_ROWS = 2 * NA_KH - 1
NA_BIAS_COLS = 2 * NA_KW - 1
NA_ROW_GROUP = 8
NA_KEY_ROWS = 16
NA_Q_TILE = NA_ROW_GROUP * GRID_W
NA_K_TILE = NA_KEY_ROWS * GRID_W

MLA_HEADS = 16
Q_LORA = 512
KV_LORA = 512
QK_NOPE = 128
QK_ROPE = 64
QK_HEAD = QK_NOPE + QK_ROPE
V_HEAD = 128
MLA_WIDTH = MLA_HEADS * V_HEAD
ROPE_THETA = 10000.0
MLA_QK_PAD = 256
MLA_IN_PAD = MLA_WIDTH + Q_LORA + KV_LORA + 256

VMEM_LIMIT_V7X = 56 * 1024 * 1024


def _params(semantics):
    return pltpu.CompilerParams(dimension_semantics=semantics, vmem_limit_bytes=VMEM_LIMIT_V7X)


def _rms(x, g):
    return x * lax.rsqrt(jnp.mean(x * x, axis=-1, keepdims=True) + RMS_EPS) * g


def _norm_matmul_kernel(x_ref, g_ref, w_ref, o_ref, h_ref):
    @pl.when(pl.program_id(1) == 0)
    def _():
        h_ref[...] = _rms(x_ref[...], g_ref[...]).astype(BF16)

    o_ref[...] = jnp.dot(h_ref[...], w_ref[...], preferred_element_type=F32).astype(o_ref.dtype)


def norm_matmul(x, g, w, *, bm, bn):
    t, d = x.shape
    n = w.shape[1]
    return pl.pallas_call(
        _norm_matmul_kernel,
        grid=(t // bm, n // bn),
        in_specs=[
            pl.BlockSpec((bm, d), lambda i, j: (i, 0)),
            pl.BlockSpec((1, d), lambda i, j: (0, 0)),
            pl.BlockSpec((d, bn), lambda i, j: (0, j)),
        ],
        out_specs=pl.BlockSpec((bm, bn), lambda i, j: (i, j)),
        out_shape=jax.ShapeDtypeStruct((t, n), BF16),
        scratch_shapes=[pltpu.VMEM((bm, d), BF16)],
        compiler_params=_params(("arbitrary", "arbitrary")),
        name="norm_matmul",
    )(x, g, w)


def _gate_out_kernel(o_ref, z_ref, w_ref, x_ref, g_ref, out_ref):
    z = z_ref[...].astype(F32)
    gate = z * (1.0 / (1.0 + jnp.exp(-z)))
    a = (o_ref[...].astype(F32) * gate).astype(BF16)
    y = jnp.dot(a, w_ref[...], preferred_element_type=F32)
    out_ref[...] = x_ref[...] + _rms(y, g_ref[...])


def gate_out(o, yz, w_out, x, g, *, bm):
    t, d = x.shape
    return pl.pallas_call(
        _gate_out_kernel,
        grid=(t // bm,),
        in_specs=[
            pl.BlockSpec((bm, d), lambda i: (i, 0)),
            pl.BlockSpec((bm, d), lambda i: (i, 0)),
            pl.BlockSpec((d, d), lambda i: (0, 0)),
            pl.BlockSpec((bm, d), lambda i: (i, 0)),
            pl.BlockSpec((1, d), lambda i: (0, 0)),
        ],
        out_specs=pl.BlockSpec((bm, d), lambda i: (i, 0)),
        out_shape=jax.ShapeDtypeStruct((t, d), F32),
        compiler_params=_params(("arbitrary",)),
        name="gate_out",
    )(o, yz, w_out, x, g)


def _na_bias_table_kernel(rpb_ref, out_ref):
    h = pl.program_id(0)
    c = lax.broadcasted_iota(jnp.int32, (GRID_W, 2 * GRID_W), 0)
    lane = lax.broadcasted_iota(jnp.int32, (GRID_W, 2 * GRID_W), 1)
    kc = lane & (GRID_W - 1)
    diff = kc - c + (NA_KW - 1)
    col_start = jnp.clip(c - NA_KW // 2, 0, GRID_W - NA_KW)
    valid = (kc >= col_start) & (kc < col_start + NA_KW)
    neg = jnp.full((GRID_W, 2 * GRID_W), NEG, F32)

    def toeplitz(dr):
        base = (h * NA_BIAS_ROWS + dr) * NA_BIAS_COLS

        def body(d, acc):
            return jnp.where(diff == d, rpb_ref[base + d], acc)

        acc = lax.fori_loop(0, NA_BIAS_COLS, body, jnp.zeros((GRID_W, 2 * GRID_W), F32))
        return jnp.where(valid, acc * LOG2E, neg)

    slabs = [toeplitz(dr) for dr in range(NA_BIAS_ROWS)]

    def slab(dr):
        return slabs[dr] if 0 <= dr < NA_BIAS_ROWS else neg

    for d2 in range(NA_BIAS_ROWS + 2):
        out_ref[0, d2] = jnp.where(lane >= GRID_W, slab(d2), slab(d2 - 1))


def na_bias_table(rpb):
    heads = rpb.shape[0]
    return pl.pallas_call(
        _na_bias_table_kernel,
        grid=(heads,),
        in_specs=[pl.BlockSpec(memory_space=pltpu.SMEM)],
        out_specs=pl.BlockSpec((1, NA_BIAS_ROWS + 2, GRID_W, 2 * GRID_W), lambda h: (h, 0, 0, 0)),
        out_shape=jax.ShapeDtypeStruct((heads, NA_BIAS_ROWS + 2, GRID_W, 2 * GRID_W), F32),
        compiler_params=_params(("arbitrary",)),
        name="na_bias_table",
    )(rpb.reshape(-1))


def _na_group_geometry(kind, rows):
    if kind == 0:
        return 0, 0
    if kind == 1:
        return NA_ROW_GROUP, NA_ROW_GROUP - NA_KH // 2
    return rows - NA_ROW_GROUP, rows - NA_KEY_ROWS


def _build_na_bias_tile(table_ref, bias_ref, kind, rows):
    r0, kr0 = _na_group_geometry(kind, rows)
    lane = lax.broadcasted_iota(jnp.int32, (GRID_W, 2 * GRID_W), 1)
    neg = jnp.full((GRID_W, 2 * GRID_W), NEG, F32)
    for i in range(NA_ROW_GROUP):
        r = r0 + i
        row_start = min(max(r - NA_KH // 2, 0), rows - NA_KH)
        for jp in range(NA_KEY_ROWS // 2):
            kl = kr0 + 2 * jp
            ok_l = row_start <= kl < row_start + NA_KH
            ok_r = row_start <= kl + 1 < row_start + NA_KH
            d2 = kl - r + (NA_KH - 1) + 1
            if ok_l or ok_r:
                tile = table_ref[0, d2]
                if not ok_l:
                    tile = jnp.where(lane >= GRID_W, tile, neg)
                elif not ok_r:
                    tile = jnp.where(lane >= GRID_W, neg, tile)
            else:
                tile = neg
            bias_ref[kind, i * GRID_W:(i + 1) * GRID_W, jp * 2 * GRID_W:(jp + 1) * 2 * GRID_W] = tile


def _na_attn_kernel(q_ref, k_ref, v_ref, table_ref, o_ref, bias_ref, *, rows):
    b = pl.program_id(1)
    g = pl.program_id(2)
    n_groups = rows // NA_ROW_GROUP

    @pl.when((b == 0) & (g == 0))
    def _():
        for kind in range(3):
            _build_na_bias_tile(table_ref, bias_ref, kind, rows)

    kr0 = jnp.clip(g * NA_ROW_GROUP - NA_KH // 2, 0, rows - NA_KEY_ROWS)
    start = pl.multiple_of(kr0 * GRID_W, GRID_W)
    kind = jnp.where(g == 0, 0, jnp.where(g == n_groups - 1, 2, 1))
    k = k_ref[pl.ds(start, NA_K_TILE), :]
    v = v_ref[pl.ds(start, NA_K_TILE), :]
    s = lax.dot_general(q_ref[...], k, (((1,), (1,)), ((), ())), preferred_element_type=F32)
    s = s + bias_ref[kind]
    m = jnp.max(s, axis=1, keepdims=True)
    p = jnp.exp2(s - m)
    l = jnp.sum(p, axis=1, keepdims=True)
    o = jnp.dot(p.astype(BF16), v, preferred_element_type=F32)
    o_ref[...] = (o * (1.0 / l)).astype(o_ref.dtype)


def na_attention(y, table, *, batch, seq):
    rows = seq // GRID_W
    n_groups = rows // NA_ROW_GROUP
    hb = NA_WIDTH // NA_HEAD_DIM
    return pl.pallas_call(
        functools.partial(_na_attn_kernel, rows=rows),
        grid=(NA_HEADS, batch, n_groups),
        in_specs=[
            pl.BlockSpec((NA_Q_TILE, NA_HEAD_DIM), lambda h, b, g: (b * n_groups + g, hb + h)),
            pl.BlockSpec((seq, NA_HEAD_DIM), lambda h, b, g: (b, 2 * hb + h)),
            pl.BlockSpec((seq, NA_HEAD_DIM), lambda h, b, g: (b, 3 * hb + h)),
            pl.BlockSpec((1, NA_BIAS_ROWS + 2, GRID_W, 2 * GRID_W), lambda h, b, g: (h, 0, 0, 0)),
        ],
        out_specs=pl.BlockSpec((NA_Q_TILE, NA_HEAD_DIM), lambda h, b, g: (b * n_groups + g, h)),
        out_shape=jax.ShapeDtypeStruct((batch * seq, NA_WIDTH), BF16),
        scratch_shapes=[pltpu.VMEM((3, NA_Q_TILE, NA_K_TILE), F32)],
        compiler_params=_params(("arbitrary", "arbitrary", "arbitrary")),
        name="na_attention",
    )(y, y, y, table)


def _rope(y2, cos_t, sin_t):
    return y2 * cos_t + pltpu.roll(y2, QK_ROPE, 1) * sin_t


def _mla_proj_kernel(cq_ref, ckv_ref, kr_ref, gq_ref, gkv_ref, cos_ref, sin_ref,
                     wq_ref, wk_ref, wvt_ref, q_ref, k_ref, vt_ref):
    hq = _rms(cq_ref[...].astype(F32), gq_ref[...]).astype(BF16)
    hkv = _rms(ckv_ref[...].astype(F32), gkv_ref[...]).astype(BF16)
    cos_t = cos_ref[...]
    sin_t = sin_ref[...]
    k_roped = _rope(kr_ref[...].astype(F32), cos_t, sin_t).astype(BF16)
    q_all = jnp.dot(hq, wq_ref[...], preferred_element_type=F32)
    kn_all = jnp.dot(hkv, wk_ref[...], preferred_element_type=F32)
    vt_all = lax.dot_general(wvt_ref[...], hkv, (((1,), (1,)), ((), ())), preferred_element_type=F32)
    for h in range(MLA_HEADS):
        q0 = h * MLA_QK_PAD
        q_ref[0, h, :, :QK_NOPE] = q_all[:, q0:q0 + QK_NOPE].astype(BF16)
        q_ref[0, h, :, QK_NOPE:] = _rope(q_all[:, q0 + QK_NOPE:q0 + MLA_QK_PAD], cos_t, sin_t).astype(BF16)
        k_ref[0, h, :, :QK_NOPE] = kn_all[:, h * QK_NOPE:(h + 1) * QK_NOPE].astype(BF16)
        k_ref[0, h, :, QK_NOPE:] = k_roped
        vt_ref[0, h] = vt_all[h * V_HEAD:(h + 1) * V_HEAD, :].astype(BF16)


def mla_projections(y, gq, gkv, cos_t, sin_t, wq, wk, wvt, *, batch, seq, bm):
    nb = seq // bm
    cq_blk = MLA_WIDTH // Q_LORA
    kr_blk = (MLA_WIDTH + Q_LORA + KV_LORA) // 128
    const2 = lambda b, i: (0, 0)
    return pl.pallas_call(
        _mla_proj_kernel,
        grid=(batch, nb),
        in_specs=[
            pl.BlockSpec((bm, Q_LORA), lambda b, i: (b * nb + i, cq_blk)),
            pl.BlockSpec((bm, KV_LORA), lambda b, i: (b * nb + i, cq_blk + 1)),
            pl.BlockSpec((bm, 128), lambda b, i: (b * nb + i, kr_blk)),
            pl.BlockSpec((1, Q_LORA), const2),
            pl.BlockSpec((1, KV_LORA), const2),
            pl.BlockSpec((bm, 128), lambda b, i: (i, 0)),
            pl.BlockSpec((bm, 128), lambda b, i: (i, 0)),
            pl.BlockSpec(wq.shape, const2),
            pl.BlockSpec(wk.shape, const2),
            pl.BlockSpec(wvt.shape, const2),
        ],
        out_specs=[
            pl.BlockSpec((1, MLA_HEADS, bm, MLA_QK_PAD), lambda b, i: (b, 0, i, 0)),
            pl.BlockSpec((1, MLA_HEADS, bm, MLA_QK_PAD), lambda b, i: (b, 0, i, 0)),
            pl.BlockSpec((1, MLA_HEADS, V_HEAD, bm), lambda b, i: (b, 0, 0, i)),
        ],
        out_shape=[
            jax.ShapeDtypeStruct((batch, MLA_HEADS, seq, MLA_QK_PAD), BF16),
            jax.ShapeDtypeStruct((batch, MLA_HEADS, seq, MLA_QK_PAD), BF16),
            jax.ShapeDtypeStruct((batch, MLA_HEADS, V_HEAD, seq), BF16),
        ],
        compiler_params=_params(("arbitrary", "arbitrary")),
        name="mla_projections",
    )(y, y, y, gq, gkv, cos_t, sin_t, wq, wk, wvt)


def _flash_kernel(q_ref, k_ref, vt_ref, o_ref, m_ref, l_ref, acc_ref, *, bkv, n_kv):
    q = q_ref[0, 0]
    m_ref[...] = jnp.full(m_ref.shape, -jnp.inf, F32)
    l_ref[...] = jnp.zeros(l_ref.shape, F32)
    acc_ref[...] = jnp.zeros(acc_ref.shape, F32)

    def body(c, carry):
        start = pl.multiple_of(c * bkv, bkv)
        kc = k_ref[0, 0, pl.ds(start, bkv), :]
        st = lax.dot_general(kc, q, (((1,), (1,)), ((), ())), preferred_element_type=F32)
        m_old = m_ref[...]
        m_new = jnp.maximum(m_old, jnp.max(st, axis=0, keepdims=True))
        alpha = jnp.exp2(m_old - m_new)
        pt = jnp.exp2(st - m_new)
        l_ref[...] = alpha * l_ref[...] + jnp.sum(pt, axis=0, keepdims=True)
        vc = vt_ref[0, 0, :, pl.ds(start, bkv)]
        pv = jnp.dot(vc, pt.astype(BF16), preferred_element_type=F32)
        acc_ref[...] = alpha * acc_ref[...] + pv
        m_ref[...] = m_new
        return carry

    lax.fori_loop(0, n_kv, body, 0)
    o_t = acc_ref[...] * (1.0 / l_ref[...])
    o_ref[0] = o_t.T.astype(o_ref.dtype)


def flash_attention(q, k, vt, *, bq, bkv):
    batch, heads, seq, dq = q.shape
    dv = vt.shape[2]
    return pl.pallas_call(
        functools.partial(_flash_kernel, bkv=bkv, n_kv=seq // bkv),
        grid=(batch, heads, seq // bq),
        in_specs=[
            pl.BlockSpec((1, 1, bq, dq), lambda b, h, i: (b, h, i, 0)),
            pl.BlockSpec((1, 1, seq, dq), lambda b, h, i: (b, h, 0, 0)),
            pl.BlockSpec((1, 1, dv, seq), lambda b, h, i: (b, h, 0, 0)),
        ],
        out_specs=pl.BlockSpec((1, bq, dv), lambda b, h, i: (b, i, h)),
        out_shape=jax.ShapeDtypeStruct((batch, seq, heads * dv), BF16),
        scratch_shapes=[
            pltpu.VMEM((1, bq), F32),
            pltpu.VMEM((1, bq), F32),
            pltpu.VMEM((dv, bq), F32),
        ],
        compiler_params=_params(("arbitrary", "arbitrary", "arbitrary")),
        name="flash_attention",
    )(q, k, vt)


def _prep_na_w_in(w):
    q, k, v, z = jnp.split(w, 4, axis=1)
    q = q * (NA_HEAD_DIM ** -0.5 * LOG2E)
    return jnp.concatenate([z, q, k, v], axis=1).astype(BF16)


def _prep_mla_w_in(w):
    cq = w[:, :Q_LORA]
    ckv = w[:, Q_LORA:Q_LORA + KV_LORA]
    kr = w[:, Q_LORA + KV_LORA:Q_LORA + KV_LORA + QK_ROPE]
    z = w[:, Q_LORA + KV_LORA + QK_ROPE:]
    kr_e, kr_o = kr[:, 0::2], kr[:, 1::2]
    pad = jnp.zeros((w.shape[0], MLA_IN_PAD - MLA_WIDTH - Q_LORA - KV_LORA - 2 * QK_ROPE), w.dtype)
    return jnp.concatenate([z, cq, ckv, kr_e, kr_o, kr_o, kr_e, pad], axis=1).astype(BF16)


def _prep_mla_w_q(w):
    w = w.reshape(Q_LORA, MLA_HEADS, QK_HEAD) * (QK_HEAD ** -0.5 * LOG2E)
    nope, rope = w[..., :QK_NOPE], w[..., QK_NOPE:]
    r_e, r_o = rope[..., 0::2], rope[..., 1::2]
    ext = jnp.concatenate([nope, r_e, r_o, r_o, r_e], axis=-1)
    return ext.reshape(Q_LORA, MLA_HEADS * MLA_QK_PAD).astype(BF16)


def _prep_mla_w_kv(w):
    w = w.reshape(KV_LORA, MLA_HEADS, QK_NOPE + V_HEAD)
    wk = w[..., :QK_NOPE].reshape(KV_LORA, MLA_HEADS * QK_NOPE)
    wvt = w[..., QK_NOPE:].reshape(KV_LORA, MLA_HEADS * V_HEAD).T
    return wk.astype(BF16), wvt.astype(BF16)


def _rope_tables(length):
    inv_freq = 1.0 / (ROPE_THETA ** (jnp.arange(0, QK_ROPE, 2, dtype=F32) / QK_ROPE))
    ang = jnp.arange(length, dtype=F32)[:, None] * inv_freq[None, :]
    c, s = jnp.cos(ang), jnp.sin(ang)
    zeros = jnp.zeros((length, QK_ROPE), F32)
    return jnp.concatenate([c, c, zeros], axis=1), jnp.concatenate([-s, s, zeros], axis=1)


def kernel(x, norm_pre, norm_post, na_w_in, na_rpb, na_w_out, mla_w_in, mla_q_norm,
           mla_w_q_b, mla_kv_norm, mla_w_kv_b, mla_w_out):
    batch, seq, d = x.shape
    xt = x.reshape(batch * seq, d)
    cos_t, sin_t = _rope_tables(seq)
    for i in range(DEPTH):
        j = i // 2
        g_pre = norm_pre[i].reshape(1, d)
        g_post = norm_post[i].reshape(1, d)
        if i % 2 == 0:
            y = norm_matmul(xt, g_pre, _prep_na_w_in(na_w_in[j]), bm=1024, bn=1024)
            table = na_bias_table(na_rpb[j])
            o = na_attention(y, table, batch=batch, seq=seq)
            w_out = na_w_out[j]
        else:
            y = norm_matmul(xt, g_pre, _prep_mla_w_in(mla_w_in[j]), bm=512, bn=MLA_IN_PAD // 2)
            wk, wvt = _prep_mla_w_kv(mla_w_kv_b[j])
            q, k, vt = mla_projections(
                y, mla_q_norm[j].reshape(1, -1), mla_kv_norm[j].reshape(1, -1), cos_t, sin_t,
                _prep_mla_w_q(mla_w_q_b[j]), wk, wvt, batch=batch, seq=seq, bm=256)
            o = flash_attention(q, k, vt, bq=512, bkv=1024).reshape(batch * seq, MLA_WIDTH)
            w_out = mla_w_out[j]
        xt = gate_out(o, y, w_out.astype(BF16), xt, g_post, bm=256)
    return xt.reshape(batch, seq, d)
```

```python
import functools
import math

import jax
import jax.numpy as jnp
from jax import lax
from jax.experimental import pallas as pl
from jax.experimental.pallas import tpu as pltpu

F32 = jnp.float32
BF16 = jnp.bfloat16

D_MODEL = 2048
DEPTH = 4
GRID_W = 64
RMS_EPS = 1e-6
LOG2E = math.log2(math.e)
NEG = -1e30

NA_HEADS = 16
NA_HEAD_DIM = 128
NA_WIDTH = NA_HEADS * NA_HEAD_DIM
NA_KH = 8
NA_KW = 16
NA_BIAS_ROWS = 2 * NA_KH - 1
NA_BIAS_COLS = 2 * NA_KW - 1
NA_ROW_GROUP = 8
NA_KEY_ROWS = 16
NA_Q_TILE = NA_ROW_GROUP * GRID_W
NA_K_TILE = NA_KEY_ROWS * GRID_W

MLA_HEADS = 16
Q_LORA = 512
KV_LORA = 512
QK_NOPE = 128
QK_ROPE = 64
QK_HEAD = QK_NOPE + QK_ROPE
V_HEAD = 128
MLA_WIDTH = MLA_HEADS * V_HEAD
ROPE_THETA = 10000.0
MLA_QK_PAD = 256
MLA_V_EXT = V_HEAD + 16
FLASH_MAX_EXCESS = 32.0
MLA_IN_PAD = MLA_WIDTH + Q_LORA + KV_LORA + 256

VMEM_LIMIT_V7X = 56 * 1024 * 1024


def _params(semantics):
    return pltpu.CompilerParams(dimension_semantics=semantics, vmem_limit_bytes=VMEM_LIMIT_V7X)


def _rms(x, g):
    return x * lax.rsqrt(jnp.mean(x * x, axis=-1, keepdims=True) + RMS_EPS) * g


def _norm_matmul_kernel(x_ref, g_ref, w_ref, o_ref, h_ref):
    @pl.when(pl.program_id(1) == 0)
    def _():
        h_ref[...] = _rms(x_ref[...], g_ref[...]).astype(BF16)

    o_ref[...] = jnp.dot(h_ref[...], w_ref[...], preferred_element_type=F32).astype(o_ref.dtype)


def norm_matmul(x, g, w, *, bm, bn):
    t, d = x.shape
    n = w.shape[1]
    return pl.pallas_call(
        _norm_matmul_kernel,
        grid=(t // bm, n // bn),
        in_specs=[
            pl.BlockSpec((bm, d), lambda i, j: (i, 0)),
            pl.BlockSpec((1, d), lambda i, j: (0, 0)),
            pl.BlockSpec((d, bn), lambda i, j: (0, j)),
        ],
        out_specs=pl.BlockSpec((bm, bn), lambda i, j: (i, j)),
        out_shape=jax.ShapeDtypeStruct((t, n), BF16),
        scratch_shapes=[pltpu.VMEM((bm, d), BF16)],
        compiler_params=_params(("arbitrary", "arbitrary")),
        name="norm_matmul",
    )(x, g, w)


def _gate_out_kernel(o_ref, z_ref, w_ref, x_ref, g_ref, out_ref):
    z = z_ref[...].astype(F32)
    gate = z * (1.0 / (1.0 + jnp.exp(-z)))
    a = (o_ref[...].astype(F32) * gate).astype(BF16)
    y = jnp.dot(a, w_ref[...], preferred_element_type=F32)
    out_ref[...] = x_ref[...] + _rms(y, g_ref[...])


def gate_out(o, yz, w_out, x, g, *, bm):
    t, d = x.shape
    return pl.pallas_call(
        _gate_out_kernel,
        grid=(t // bm,),
        in_specs=[
            pl.BlockSpec((bm, d), lambda i: (i, 0)),
            pl.BlockSpec((bm, d), lambda i: (i, 0)),
            pl.BlockSpec((d, d), lambda i: (0, 0)),
            pl.BlockSpec((bm, d), lambda i: (i, 0)),
            pl.BlockSpec((1, d), lambda i: (0, 0)),
        ],
        out_specs=pl.BlockSpec((bm, d), lambda i: (i, 0)),
        out_shape=jax.ShapeDtypeStruct((t, d), F32),
        compiler_params=_params(("arbitrary",)),
        name="gate_out",
    )(o, yz, w_out, x, g)


def _na_bias_table_kernel(rpb_ref, out_ref):
    h = pl.program_id(0)
    c = lax.broadcasted_iota(jnp.int32, (GRID_W, 2 * GRID_W), 0)
    lane = lax.broadcasted_iota(jnp.int32, (GRID_W, 2 * GRID_W), 1)
    kc = lane & (GRID_W - 1)
    diff = kc - c + (NA_KW - 1)
    col_start = jnp.clip(c - NA_KW // 2, 0, GRID_W - NA_KW)
    valid = (kc >= col_start) & (kc < col_start + NA_KW)
    neg = jnp.full((GRID_W, 2 * GRID_W), NEG, F32)

    def toeplitz(dr):
        base = (h * NA_BIAS_ROWS + dr) * NA_BIAS_COLS

        def body(d, acc):
            return jnp.where(diff == d, rpb_ref[base + d], acc)

        acc = lax.fori_loop(0, NA_BIAS_COLS, body, jnp.zeros((GRID_W, 2 * GRID_W), F32))
        return jnp.where(valid, acc * LOG2E, neg)

    slabs = [toeplitz(dr) for dr in range(NA_BIAS_ROWS)]

    def slab(dr):
        return slabs[dr] if 0 <= dr < NA_BIAS_ROWS else neg

    for d2 in range(NA_BIAS_ROWS + 2):
        out_ref[0, d2] = jnp.where(lane >= GRID_W, slab(d2), slab(d2 - 1))


def na_bias_table(rpb):
    heads = rpb.shape[0]
    return pl.pallas_call(
        _na_bias_table_kernel,
        grid=(heads,),
        in_specs=[pl.BlockSpec(memory_space=pltpu.SMEM)],
        out_specs=pl.BlockSpec((1, NA_BIAS_ROWS + 2, GRID_W, 2 * GRID_W), lambda h: (h, 0, 0, 0)),
        out_shape=jax.ShapeDtypeStruct((heads, NA_BIAS_ROWS + 2, GRID_W, 2 * GRID_W), F32),
        compiler_params=_params(("arbitrary",)),
        name="na_bias_table",
    )(rpb.reshape(-1))


def _na_group_geometry(kind, rows):
    if kind == 0:
        return 0, 0
    if kind == 1:
        return NA_ROW_GROUP, NA_ROW_GROUP - NA_KH // 2
    return rows - NA_ROW_GROUP, rows - NA_KEY_ROWS


def _build_na_bias_tile(table_ref, bias_ref, kind, rows):
    r0, kr0 = _na_group_geometry(kind, rows)
    lane = lax.broadcasted_iota(jnp.int32, (GRID_W, 2 * GRID_W), 1)
    neg = jnp.full((GRID_W, 2 * GRID_W), NEG, F32)
    for i in range(NA_ROW_GROUP):
        r = r0 + i
        row_start = min(max(r - NA_KH // 2, 0), rows - NA_KH)
        for jp in range(NA_KEY_ROWS // 2):
            kl = kr0 + 2 * jp
            ok_l = row_start <= kl < row_start + NA_KH
            ok_r = row_start <= kl + 1 < row_start + NA_KH
            d2 = kl - r + (NA_KH - 1) + 1
            if ok_l or ok_r:
                tile = table_ref[0, d2]
                if not ok_l:
                    tile = jnp.where(lane >= GRID_W, tile, neg)
                elif not ok_r:
                    tile = jnp.where(lane >= GRID_W, neg, tile)
            else:
                tile = neg
            bias_ref[kind, i * GRID_W:(i + 1) * GRID_W, jp * 2 * GRID_W:(jp + 1) * 2 * GRID_W] = tile


def _na_attn_kernel(q_ref, k_ref, v_ref, table_ref, o_ref, bias_ref, *, rows):
    b = pl.program_id(1)
    g = pl.program_id(2)
    n_groups = rows // NA_ROW_GROUP

    @pl.when((b == 0) & (g == 0))
    def _():
        for kind in range(3):
            _build_na_bias_tile(table_ref, bias_ref, kind, rows)

    kr0 = jnp.clip(g * NA_ROW_GROUP - NA_KH // 2, 0, rows - NA_KEY_ROWS)
    start = pl.multiple_of(kr0 * GRID_W, GRID_W)
    kind = jnp.where(g == 0, 0, jnp.where(g == n_groups - 1, 2, 1))
    k = k_ref[pl.ds(start, NA_K_TILE), :]
    v = v_ref[pl.ds(start, NA_K_TILE), :]
    s = lax.dot_general(q_ref[...], k, (((1,), (1,)), ((), ())), preferred_element_type=F32)
    s = s + bias_ref[kind]
    m = jnp.max(s, axis=1, keepdims=True)
    p = jnp.exp2(s - m)
    l = jnp.sum(p, axis=1, keepdims=True)
    o = jnp.dot(p.astype(BF16), v, preferred_element_type=F32)
    o_ref[...] = (o * (1.0 / l)).astype(o_ref.dtype)


def na_attention(y, table, *, batch, seq):
    rows = seq // GRID_W
    n_groups = rows // NA_ROW_GROUP
    hb = NA_WIDTH // NA_HEAD_DIM
    return pl.pallas_call(
        functools.partial(_na_attn_kernel, rows=rows),
        grid=(NA_HEADS, batch, n_groups),
        in_specs=[
            pl.BlockSpec((NA_Q_TILE, NA_HEAD_DIM), lambda h, b, g: (b * n_groups + g, hb + h)),
            pl.BlockSpec((seq, NA_HEAD_DIM), lambda h, b, g: (b, 2 * hb + h)),
            pl.BlockSpec((seq, NA_HEAD_DIM), lambda h, b, g: (b, 3 * hb + h)),
            pl.BlockSpec((1, NA_BIAS_ROWS + 2, GRID_W, 2 * GRID_W), lambda h, b, g: (h, 0, 0, 0)),
        ],
        out_specs=pl.BlockSpec((NA_Q_TILE, NA_HEAD_DIM), lambda h, b, g: (b * n_groups + g, h)),
        out_shape=jax.ShapeDtypeStruct((batch * seq, NA_WIDTH), BF16),
        scratch_shapes=[pltpu.VMEM((3, NA_Q_TILE, NA_K_TILE), F32)],
        compiler_params=_params(("arbitrary", "arbitrary", "arbitrary")),
        name="na_attention",
    )(y, y, y, table)


def _mla_proj_kernel(cq_ref, ckv_ref, kr_ref, gq_ref, gkv_ref, cos_ref, sin_ref, cos_t_ref, sin_t_ref,
                     wqt_ref, wk_ref, wvt_ref, qt_ref, k_ref, vt_ref):
    hq = _rms(cq_ref[...].astype(F32), gq_ref[...]).astype(BF16)
    hkv = _rms(ckv_ref[...].astype(F32), gkv_ref[...]).astype(BF16)
    kr = kr_ref[...].astype(F32)
    k_roped = (kr * cos_ref[...] + pltpu.roll(kr, QK_ROPE, 1) * sin_ref[...]).astype(BF16)
    cos_t = cos_t_ref[...]
    sin_t = sin_t_ref[...]
    nt = (((1,), (1,)), ((), ()))
    qt_all = lax.dot_general(wqt_ref[...], hq, nt, preferred_element_type=F32)
    kn_all = jnp.dot(hkv, wk_ref[...], preferred_element_type=F32)
    vt_all = lax.dot_general(wvt_ref[...], hkv, nt, preferred_element_type=F32)
    ones = jnp.ones((MLA_V_EXT - V_HEAD, hq.shape[0]), BF16)
    for h in range(MLA_HEADS):
        r0 = h * MLA_QK_PAD
        qt_ref[0, h, :QK_NOPE, :] = qt_all[r0:r0 + QK_NOPE].astype(BF16)
        y2 = qt_all[r0 + QK_NOPE:r0 + MLA_QK_PAD]
        partner = jnp.concatenate([y2[QK_ROPE:], y2[:QK_ROPE]], axis=0)
        qt_ref[0, h, QK_NOPE:, :] = (y2 * cos_t + partner * sin_t).astype(BF16)
        k_ref[0, h, :, :QK_NOPE] = kn_all[:, h * QK_NOPE:(h + 1) * QK_NOPE].astype(BF16)
        k_ref[0, h, :, QK_NOPE:] = k_roped
        vt_ref[0, h, :V_HEAD, :] = vt_all[h * V_HEAD:(h + 1) * V_HEAD, :].astype(BF16)
        vt_ref[0, h, V_HEAD:, :] = ones


def mla_projections(y, gq, gkv, cos, sin, wqt, wk, wvt, *, batch, seq, bm):
    nb = seq // bm
    cq_blk = MLA_WIDTH // Q_LORA
    kr_blk = (MLA_WIDTH + Q_LORA + KV_LORA) // 128
    const2 = lambda b, i: (0, 0)
    return pl.pallas_call(
        _mla_proj_kernel,
        grid=(batch, nb),
        in_specs=[
            pl.BlockSpec((bm, Q_LORA), lambda b, i: (b * nb + i, cq_blk)),
            pl.BlockSpec((bm, KV_LORA), lambda b, i: (b * nb + i, cq_blk + 1)),
            pl.BlockSpec((bm, 128), lambda b, i: (b * nb + i, kr_blk)),
            pl.BlockSpec((1, Q_LORA), const2),
            pl.BlockSpec((1, KV_LORA), const2),
            pl.BlockSpec((bm, 128), lambda b, i: (i, 0)),
            pl.BlockSpec((bm, 128), lambda b, i: (i, 0)),
            pl.BlockSpec((128, bm), lambda b, i: (0, i)),
            pl.BlockSpec((128, bm), lambda b, i: (0, i)),
            pl.BlockSpec(wqt.shape, const2),
            pl.BlockSpec(wk.shape, const2),
            pl.BlockSpec(wvt.shape, const2),
        ],
        out_specs=[
            pl.BlockSpec((1, MLA_HEADS, MLA_QK_PAD, bm), lambda b, i: (b, 0, 0, i)),
            pl.BlockSpec((1, MLA_HEADS, bm, MLA_QK_PAD), lambda b, i: (b, 0, i, 0)),
            pl.BlockSpec((1, MLA_HEADS, MLA_V_EXT, bm), lambda b, i: (b, 0, 0, i)),
        ],
        out_shape=[
            jax.ShapeDtypeStruct((batch, MLA_HEADS, MLA_QK_PAD, seq), BF16),
            jax.ShapeDtypeStruct((batch, MLA_HEADS, seq, MLA_QK_PAD), BF16),
            jax.ShapeDtypeStruct((batch, MLA_HEADS, MLA_V_EXT, seq), BF16),
        ],
        compiler_params=_params(("arbitrary", "arbitrary")),
        name="mla_projections",
    )(y, y, y, gq, gkv, cos, sin, cos.T, sin.T, wqt, wk, wvt)


def _flash_store(o_ref, acc):
    dv = o_ref.shape[2]
    o_t = acc[:dv] * (1.0 / acc[dv:dv + 1])
    o_ref[0] = o_t.T.astype(o_ref.dtype)


def _flash_kernel(qt_ref, k_ref, vt_ref, o_ref, m_ref, acc_ref, *, bkv, n_kv, first):
    qt = qt_ref[0, 0]
    seq = k_ref.shape[2]

    def scores(lo, hi):
        return jnp.dot(k_ref[0, 0, lo:hi, :], qt, preferred_element_type=F32)

    def weighted(lo, hi, pt):
        return jnp.dot(vt_ref[0, 0, :, lo:hi], pt.astype(BF16), preferred_element_type=F32)

    st = scores(0, first)
    m0 = jnp.max(st, axis=0, keepdims=True)
    acc = weighted(0, first, jnp.exp2(st - m0))
    seen = m0
    for lo in range(first, seq, bkv):
        hi = min(lo + bkv, seq)
        st = scores(lo, hi)
        seen = jnp.maximum(seen, jnp.max(st, axis=0, keepdims=True))
        acc = acc + weighted(lo, hi, jnp.exp2(st - m0))
    _flash_store(o_ref, acc)

    @pl.when(jnp.max(seen - m0) > FLASH_MAX_EXCESS)
    def _():
        m_ref[...] = jnp.full(m_ref.shape, -jnp.inf, F32)
        acc_ref[...] = jnp.zeros(acc_ref.shape, F32)

        def body(c, carry):
            start = pl.multiple_of(c * bkv, bkv)
            s = jnp.dot(k_ref[0, 0, pl.ds(start, bkv), :], qt, preferred_element_type=F32)
            m_old = m_ref[...]
            m_new = jnp.maximum(m_old, jnp.max(s, axis=0, keepdims=True))
            pv = jnp.dot(vt_ref[0, 0, :, pl.ds(start, bkv)], jnp.exp2(s - m_new).astype(BF16),
                         preferred_element_type=F32)
            acc_ref[...] = jnp.exp2(m_old - m_new) * acc_ref[...] + pv
            m_ref[...] = m_new
            return carry

        lax.fori_loop(0, n_kv, body, 0)
        _flash_store(o_ref, acc_ref[...])


def flash_attention(qt, k, vt, *, bq, bkv, first):
    batch, heads, dq, seq = qt.shape
    dv_ext = vt.shape[2]
    return pl.pallas_call(
        functools.partial(_flash_kernel, bkv=bkv, n_kv=seq // bkv, first=first),
        grid=(batch, heads, seq // bq),
        in_specs=[
            pl.BlockSpec((1, 1, dq, bq), lambda b, h, i: (b, h, 0, i)),
            pl.BlockSpec((1, 1, seq, dq), lambda b, h, i: (b, h, 0, 0)),
            pl.BlockSpec((1, 1, dv_ext, seq), lambda b, h, i: (b, h, 0, 0)),
        ],
        out_specs=pl.BlockSpec((1, bq, V_HEAD), lambda b, h, i: (b, i, h)),
        out_shape=jax.ShapeDtypeStruct((batch, seq, heads * V_HEAD), BF16),
        scratch_shapes=[pltpu.VMEM((1, bq), F32), pltpu.VMEM((dv_ext, bq), F32)],
        compiler_params=_params(("arbitrary", "arbitrary", "arbitrary")),
        name="flash_attention",
    )(qt, k, vt)


def _prep_na_w_in(w):
    q, k, v, z = jnp.split(w, 4, axis=1)
    q = q * (NA_HEAD_DIM ** -0.5 * LOG2E)
    return jnp.concatenate([z, q, k, v], axis=1).astype(BF16)


def _prep_mla_w_in(w):
    cq = w[:, :Q_LORA]
    ckv = w[:, Q_LORA:Q_LORA + KV_LORA]
    kr = w[:, Q_LORA + KV_LORA:Q_LORA + KV_LORA + QK_ROPE]
    z = w[:, Q_LORA + KV_LORA + QK_ROPE:]
    kr_e, kr_o = kr[:, 0::2], kr[:, 1::2]
    pad = jnp.zeros((w.shape[0], MLA_IN_PAD - MLA_WIDTH - Q_LORA - KV_LORA - 2 * QK_ROPE), w.dtype)
    return jnp.concatenate([z, cq, ckv, kr_e, kr_o, kr_o, kr_e, pad], axis=1).astype(BF16)


def _prep_mla_w_q(w):
    w = w.reshape(Q_LORA, MLA_HEADS, QK_HEAD) * (QK_HEAD ** -0.5 * LOG2E)
    nope, rope = w[..., :QK_NOPE], w[..., QK_NOPE:]
    r_e, r_o = rope[..., 0::2], rope[..., 1::2]
    ext = jnp.concatenate([nope, r_e, r_o, r_o, r_e], axis=-1)
    return ext.reshape(Q_LORA, MLA_HEADS * MLA_QK_PAD).astype(BF16)


def _prep_mla_w_kv(w):
    w = w.reshape(KV_LORA, MLA_HEADS, QK_NOPE + V_HEAD)
    wk = w[..., :QK_NOPE].reshape(KV_LORA, MLA_HEADS * QK_NOPE)
    wvt = w[..., QK_NOPE:].reshape(KV_LORA, MLA_HEADS * V_HEAD).T
    return wk.astype(BF16), wvt.astype(BF16)


def _rope_tables(length):
    inv_freq = 1.0 / (ROPE_THETA ** (jnp.arange(0, QK_ROPE, 2, dtype=F32) / QK_ROPE))
    ang = jnp.arange(length, dtype=F32)[:, None] * inv_freq[None, :]
    c, s = jnp.cos(ang), jnp.sin(ang)
    zeros = jnp.zeros((length, QK_ROPE), F32)
    return jnp.concatenate([c, c, zeros], axis=1), jnp.concatenate([-s, s, zeros], axis=1)


def kernel(x, norm_pre, norm_post, na_w_in, na_rpb, na_w_out, mla_w_in, mla_q_norm,
           mla_w_q_b, mla_kv_norm, mla_w_kv_b, mla_w_out):
    batch, seq, d = x.shape
    xt = x.reshape(batch * seq, d)
    cos_t, sin_t = _rope_tables(seq)
    for i in range(DEPTH):
        j = i // 2
        g_pre = norm_pre[i].reshape(1, d)
        g_post = norm_post[i].reshape(1, d)
        if i % 2 == 0:
            y = norm_matmul(xt, g_pre, _prep_na_w_in(na_w_in[j]), bm=1024, bn=1024)
            table = na_bias_table(na_rpb[j])
            o = na_attention(y, table, batch=batch, seq=seq)
            w_out = na_w_out[j]
        else:
            y = norm_matmul(xt, g_pre, _prep_mla_w_in(mla_w_in[j]), bm=512, bn=MLA_IN_PAD // 2)
            wk, wvt = _prep_mla_w_kv(mla_w_kv_b[j])
            qt, k, vt = mla_projections(
                y, mla_q_norm[j].reshape(1, -1), mla_kv_norm[j].reshape(1, -1), cos_t, sin_t,
                _prep_mla_w_q(mla_w_q_b[j]).T, wk, wvt, batch=batch, seq=seq, bm=256)
            o = flash_attention(qt, k, vt, bq=512, bkv=1024, first=256).reshape(batch * seq, MLA_WIDTH)
            w_out = mla_w_out[j]
        xt = gate_out(o, y, w_out.astype(BF16), xt, g_post, bm=256)
    return xt.reshape(batch, seq, d)
```

```python
import functools
import math

import jax
import jax.numpy as jnp
from jax import lax
from jax.experimental import pallas as pl
from jax.experimental.pallas import tpu as pltpu

F32 = jnp.float32
BF16 = jnp.bfloat16

D_MODEL = 2048
DEPTH = 4
GRID_W = 64
RMS_EPS = 1e-6
LOG2E = math.log2(math.e)
NEG = -1e30

NA_HEADS = 16
NA_HEAD_DIM = 128
NA_WIDTH = NA_HEADS * NA_HEAD_DIM
NA_KH = 8
NA_KW = 16
NA_BIAS_ROWS = 2 * NA_KH - 1
NA_BIAS_COLS = 2 * NA_KW - 1
NA_TABLE_SLABS = NA_BIAS_ROWS + 1
NA_ROW_GROUP = 8
NA_KEY_ROWS = 16
NA_Q_TILE = NA_ROW_GROUP * GRID_W
NA_K_TILE = NA_KEY_ROWS * GRID_W
NA_SAFE_ROWS = (6, 10)
NA_MAX_EXCESS = 32.0

MLA_HEADS = 16
Q_LORA = 512
KV_LORA = 512
QK_NOPE = 128
QK_ROPE = 64
QK_HEAD = QK_NOPE + QK_ROPE
V_HEAD = 128
MLA_WIDTH = MLA_HEADS * V_HEAD
ROPE_THETA = 10000.0
MLA_QK_PAD = 256
MLA_V_EXT = V_HEAD + 16
FLASH_MAX_EXCESS = 32.0
MLA_IN_PAD = MLA_WIDTH + Q_LORA + KV_LORA + 256

VMEM_LIMIT_V7X = 56 * 1024 * 1024


def _params(semantics):
    return pltpu.CompilerParams(dimension_semantics=semantics, vmem_limit_bytes=VMEM_LIMIT_V7X)


def _rms(x, g):
    return x * lax.rsqrt(jnp.mean(x * x, axis=-1, keepdims=True) + RMS_EPS) * g


def _norm_matmul_kernel(x_ref, g_ref, w_ref, o_ref, h_ref):
    @pl.when(pl.program_id(1) == 0)
    def _():
        h_ref[...] = _rms(x_ref[...], g_ref[...]).astype(BF16)

    o_ref[...] = jnp.dot(h_ref[...], w_ref[...], preferred_element_type=F32).astype(o_ref.dtype)


def norm_matmul(x, g, w, *, bm, bn):
    t, d = x.shape
    n = w.shape[1]
    return pl.pallas_call(
        _norm_matmul_kernel,
        grid=(t // bm, n // bn),
        in_specs=[
            pl.BlockSpec((bm, d), lambda i, j: (i, 0)),
            pl.BlockSpec((1, d), lambda i, j: (0, 0)),
            pl.BlockSpec((d, bn), lambda i, j: (0, j)),
        ],
        out_specs=pl.BlockSpec((bm, bn), lambda i, j: (i, j)),
        out_shape=jax.ShapeDtypeStruct((t, n), BF16),
        scratch_shapes=[pltpu.VMEM((bm, d), BF16)],
        compiler_params=_params(("arbitrary", "arbitrary")),
        name="norm_matmul",
    )(x, g, w)


def _norm_matmul_t_kernel(x_ref, g_ref, wt_ref, o_ref, h_ref):
    @pl.when(pl.program_id(1) == 0)
    def _():
        h_ref[...] = _rms(x_ref[...], g_ref[...]).astype(BF16)

    o_ref[...] = lax.dot_general(wt_ref[...], h_ref[...], (((1,), (1,)), ((), ())),
                                 preferred_element_type=F32).astype(o_ref.dtype)


def norm_matmul_t(x, g, wt, *, bm, bn):
    t, d = x.shape
    n = wt.shape[0]
    return pl.pallas_call(
        _norm_matmul_t_kernel,
        grid=(t // bm, n // bn),
        in_specs=[
            pl.BlockSpec((bm, d), lambda i, j: (i, 0)),
            pl.BlockSpec((1, d), lambda i, j: (0, 0)),
            pl.BlockSpec((bn, d), lambda i, j: (j, 0)),
        ],
        out_specs=pl.BlockSpec((bn, bm), lambda i, j: (j, i)),
        out_shape=jax.ShapeDtypeStruct((n, t), BF16),
        scratch_shapes=[pltpu.VMEM((bm, d), BF16)],
        compiler_params=_params(("arbitrary", "arbitrary")),
        name="norm_matmul_t",
    )(x, g, wt)


def _gate_out_kernel(o_ref, z_ref, w_ref, x_ref, g_ref, out_ref):
    z = z_ref[...].astype(F32)
    gate = z * (1.0 / (1.0 + jnp.exp(-z)))
    a = (o_ref[...].astype(F32) * gate).astype(BF16)
    y = jnp.dot(a, w_ref[...], preferred_element_type=F32)
    out_ref[...] = x_ref[...] + _rms(y, g_ref[...])


def gate_out(o, yz, w_out, x, g, *, bm):
    t, d = x.shape
    return pl.pallas_call(
        _gate_out_kernel,
        grid=(t // bm,),
        in_specs=[
            pl.BlockSpec((bm, d), lambda i: (i, 0)),
            pl.BlockSpec((bm, d), lambda i: (i, 0)),
            pl.BlockSpec((d, d), lambda i: (0, 0)),
            pl.BlockSpec((bm, d), lambda i: (i, 0)),
            pl.BlockSpec((1, d), lambda i: (0, 0)),
        ],
        out_specs=pl.BlockSpec((bm, d), lambda i: (i, 0)),
        out_shape=jax.ShapeDtypeStruct((t, d), F32),
        compiler_params=_params(("arbitrary",)),
        name="gate_out",
    )(o, yz, w_out, x, g)


def _na_bias_table_kernel(rpb_ref, out_ref):
    h = pl.program_id(0)
    kc = lax.broadcasted_iota(jnp.int32, (GRID_W, 2 * GRID_W), 0)
    lane = lax.broadcasted_iota(jnp.int32, (GRID_W, 2 * GRID_W), 1)
    c = lane & (GRID_W - 1)
    diff = kc - c + (NA_KW - 1)
    col_start = jnp.clip(c - NA_KW // 2, 0, GRID_W - NA_KW)
    valid = (kc >= col_start) & (kc < col_start + NA_KW)
    neg = jnp.full((GRID_W, 2 * GRID_W), NEG, F32)

    def toeplitz(dr):
        base = (h * NA_BIAS_ROWS + dr) * NA_BIAS_COLS

        def body(d, acc):
            return jnp.where(diff == d, rpb_ref[base + d], acc)

        acc = lax.fori_loop(0, NA_BIAS_COLS, body, jnp.zeros((GRID_W, 2 * GRID_W), F32))
        return jnp.where(valid, acc * LOG2E, neg)

    slabs = [toeplitz(dr) for dr in range(NA_BIAS_ROWS)]

    def slab(dr):
        return slabs[dr] if 0 <= dr < NA_BIAS_ROWS else neg

    for t in range(NA_TABLE_SLABS):
        out_ref[0, t] = jnp.where(lane >= GRID_W, slab(t - 1), slab(t))


def na_bias_table(rpb):
    heads = rpb.shape[0]
    return pl.pallas_call(
        _na_bias_table_kernel,
        grid=(heads,),
        in_specs=[pl.BlockSpec(memory_space=pltpu.SMEM)],
        out_specs=pl.BlockSpec((1, NA_TABLE_SLABS, GRID_W, 2 * GRID_W), lambda h: (h, 0, 0, 0)),
        out_shape=jax.ShapeDtypeStruct((heads, NA_TABLE_SLABS, GRID_W, 2 * GRID_W), F32),
        compiler_params=_params(("arbitrary",)),
        name="na_bias_table",
    )(rpb.reshape(-1))


def _na_group_geometry(kind, rows):
    if kind == 0:
        return 0, 0
    if kind == 1:
        return NA_ROW_GROUP, NA_ROW_GROUP - NA_KH // 2
    return rows - NA_ROW_GROUP, rows - NA_KEY_ROWS


def _build_na_bias_tile(table_ref, bias_ref, kind, rows):
    r0, kr0 = _na_group_geometry(kind, rows)
    lane = lax.broadcasted_iota(jnp.int32, (GRID_W, 2 * GRID_W), 1)
    neg = jnp.full((GRID_W, 2 * GRID_W), NEG, F32)

    def in_window(r, kr):
        row_start = min(max(r - NA_KH // 2, 0), rows - NA_KH)
        return row_start <= kr < row_start + NA_KH

    for i in range(NA_ROW_GROUP):
        assert any(in_window(r0 + i, kr0 + j) for j in range(*NA_SAFE_ROWS))
    for j in range(NA_KEY_ROWS):
        kr = kr0 + j
        for ip in range(NA_ROW_GROUP // 2):
            r_l = r0 + 2 * ip
            ok_l = in_window(r_l, kr)
            ok_r = in_window(r_l + 1, kr)
            t = kr - r_l + (NA_KH - 1)
            if ok_l or ok_r:
                tile = table_ref[0, t]
                if not ok_l:
                    tile = jnp.where(lane >= GRID_W, tile, neg)
                elif not ok_r:
                    tile = jnp.where(lane >= GRID_W, neg, tile)
            else:
                tile = neg
            bias_ref[kind, j * GRID_W:(j + 1) * GRID_W, ip * 2 * GRID_W:(ip + 1) * 2 * GRID_W] = tile


def _na_attn_kernel(qt_ref, k_ref, vt_ref, table_ref, o_ref, bias_ref, *, rows, groups_per_step):
    b = pl.program_id(1)
    step = pl.program_id(2)
    n_groups = rows // NA_ROW_GROUP

    @pl.when((b == 0) & (step == 0))
    def _():
        for kind in range(3):
            _build_na_bias_tile(table_ref, bias_ref, kind, rows)

    safe_lo, safe_hi = NA_SAFE_ROWS[0] * GRID_W, NA_SAFE_ROWS[1] * GRID_W
    ones = jnp.ones((16, NA_K_TILE), BF16)

    def group(u):
        g = step * groups_per_step + u
        kr0 = jnp.clip(g * NA_ROW_GROUP - NA_KH // 2, 0, rows - NA_KEY_ROWS)
        start = pl.multiple_of(kr0 * GRID_W, NA_KH // 2 * GRID_W)
        kind = jnp.where(g == 0, 0, jnp.where(g == n_groups - 1, 2, 1))
        qt = qt_ref[:, u * NA_Q_TILE:(u + 1) * NA_Q_TILE]
        k = k_ref[pl.ds(start, NA_K_TILE), :]
        vt_ext = jnp.concatenate([vt_ref[:, pl.ds(start, NA_K_TILE)], ones], axis=0)

        def scores(lo, hi):
            return jnp.dot(k[lo:hi], qt, preferred_element_type=F32) + bias_ref[kind, lo:hi, :]

        def weighted(lo, hi, pt):
            return jnp.dot(vt_ext[:, lo:hi], pt.astype(BF16), preferred_element_type=F32)

        def store(acc):
            o_t = acc[:NA_HEAD_DIM] * (1.0 / acc[NA_HEAD_DIM:NA_HEAD_DIM + 1])
            o_ref[u * NA_Q_TILE:(u + 1) * NA_Q_TILE, :] = o_t.T.astype(o_ref.dtype)

        return scores, weighted, store

    excess = []
    for u in range(groups_per_step):
        scores, weighted, store = group(u)
        st = scores(safe_lo, safe_hi)
        m0 = jnp.max(st, axis=0, keepdims=True)
        acc = weighted(safe_lo, safe_hi, jnp.exp2(st - m0))
        seen = m0
        for lo, hi in ((0, safe_lo), (safe_hi, NA_K_TILE)):
            st = scores(lo, hi)
            seen = jnp.maximum(seen, jnp.max(st, axis=0, keepdims=True))
            acc = acc + weighted(lo, hi, jnp.exp2(st - m0))
        store(acc)
        excess.append(seen - m0)

    @pl.when(jnp.max(jnp.concatenate(excess, axis=0)) > NA_MAX_EXCESS)
    def _():
        for u in range(groups_per_step):
            scores, weighted, store = group(u)
            s_all = scores(0, NA_K_TILE)
            m = jnp.max(s_all, axis=0, keepdims=True)
            store(weighted(0, NA_K_TILE, jnp.exp2(s_all - m)))


def na_attention(zk, qvt, table, *, batch, seq, groups_per_step):
    rows = seq // GRID_W
    n_steps = rows // NA_ROW_GROUP // groups_per_step
    bq = NA_Q_TILE * groups_per_step
    hb = NA_WIDTH // NA_HEAD_DIM
    return pl.pallas_call(
        functools.partial(_na_attn_kernel, rows=rows, groups_per_step=groups_per_step),
        grid=(NA_HEADS, batch, n_steps),
        in_specs=[
            pl.BlockSpec((NA_HEAD_DIM, bq), lambda h, b, s: (h, b * n_steps + s)),
            pl.BlockSpec((seq, NA_HEAD_DIM), lambda h, b, s: (b, hb + h)),
            pl.BlockSpec((NA_HEAD_DIM, seq), lambda h, b, s: (hb + h, b)),
            pl.BlockSpec((1, NA_TABLE_SLABS, GRID_W, 2 * GRID_W), lambda h, b, s: (h, 0, 0, 0)),
        ],
        out_specs=pl.BlockSpec((bq, NA_HEAD_DIM), lambda h, b, s: (b * n_steps + s, h)),
        out_shape=jax.ShapeDtypeStruct((batch * seq, NA_WIDTH), BF16),
        scratch_shapes=[pltpu.VMEM((3, NA_K_TILE, NA_Q_TILE), F32)],
        compiler_params=_params(("arbitrary", "arbitrary", "arbitrary")),
        name="na_attention",
    )(qvt, zk, qvt, table)


def _mla_proj_kernel(cq_ref, ckv_ref, kr_ref, gq_ref, gkv_ref, cos_ref, sin_ref, cos_t_ref, sin_t_ref,
                     wqt_ref, wk_ref, wvt_ref, qt_ref, k_ref, vt_ref):
    hq = _rms(cq_ref[...].astype(F32), gq_ref[...]).astype(BF16)
    hkv = _rms(ckv_ref[...].astype(F32), gkv_ref[...]).astype(BF16)
    kr = kr_ref[...].astype(F32)
    k_roped = (kr * cos_ref[...] + pltpu.roll(kr, QK_ROPE, 1) * sin_ref[...]).astype(BF16)
    cos_t = cos_t_ref[...]
    sin_t = sin_t_ref[...]
    nt = (((1,), (1,)), ((), ()))
    qt_all = lax.dot_general(wqt_ref[...], hq, nt, preferred_element_type=F32)
    kn_all = jnp.dot(hkv, wk_ref[...], preferred_element_type=F32)
    vt_all = lax.dot_general(wvt_ref[...], hkv, nt, preferred_element_type=F32)
    ones = jnp.ones((MLA_V_EXT - V_HEAD, hq.shape[0]), BF16)
    for h in range(MLA_HEADS):
        r0 = h * MLA_QK_PAD
        qt_ref[0, h, :QK_NOPE, :] = qt_all[r0:r0 + QK_NOPE].astype(BF16)
        y2 = qt_all[r0 + QK_NOPE:r0 + MLA_QK_PAD]
        partner = jnp.concatenate([y2[QK_ROPE:], y2[:QK_ROPE]], axis=0)
        qt_ref[0, h, QK_NOPE:, :] = (y2 * cos_t + partner * sin_t).astype(BF16)
        k_ref[0, h, :, :QK_NOPE] = kn_all[:, h * QK_NOPE:(h + 1) * QK_NOPE].astype(BF16)
        k_ref[0, h, :, QK_NOPE:] = k_roped
        vt_ref[0, h, :V_HEAD, :] = vt_all[h * V_HEAD:(h + 1) * V_HEAD, :].astype(BF16)
        vt_ref[0, h, V_HEAD:, :] = ones


def mla_projections(y, gq, gkv, cos, sin, wqt, wk, wvt, *, batch, seq, bm):
    nb = seq // bm
    cq_blk = MLA_WIDTH // Q_LORA
    kr_blk = (MLA_WIDTH + Q_LORA + KV_LORA) // 128
    const2 = lambda b, i: (0, 0)
    return pl.pallas_call(
        _mla_proj_kernel,
        grid=(batch, nb),
        in_specs=[
            pl.BlockSpec((bm, Q_LORA), lambda b, i: (b * nb + i, cq_blk)),
            pl.BlockSpec((bm, KV_LORA), lambda b, i: (b * nb + i, cq_blk + 1)),
            pl.BlockSpec((bm, 128), lambda b, i: (b * nb + i, kr_blk)),
            pl.BlockSpec((1, Q_LORA), const2),
            pl.BlockSpec((1, KV_LORA), const2),
            pl.BlockSpec((bm, 128), lambda b, i: (i, 0)),
            pl.BlockSpec((bm, 128), lambda b, i: (i, 0)),
            pl.BlockSpec((128, bm), lambda b, i: (0, i)),
            pl.BlockSpec((128, bm), lambda b, i: (0, i)),
            pl.BlockSpec(wqt.shape, const2),
            pl.BlockSpec(wk.shape, const2),
            pl.BlockSpec(wvt.shape, const2),
        ],
        out_specs=[
            pl.BlockSpec((1, MLA_HEADS, MLA_QK_PAD, bm), lambda b, i: (b, 0, 0, i)),
            pl.BlockSpec((1, MLA_HEADS, bm, MLA_QK_PAD), lambda b, i: (b, 0, i, 0)),
            pl.BlockSpec((1, MLA_HEADS, MLA_V_EXT, bm), lambda b, i: (b, 0, 0, i)),
        ],
        out_shape=[
            jax.ShapeDtypeStruct((batch, MLA_HEADS, MLA_QK_PAD, seq), BF16),
            jax.ShapeDtypeStruct((batch, MLA_HEADS, seq, MLA_QK_PAD), BF16),
            jax.ShapeDtypeStruct((batch, MLA_HEADS, MLA_V_EXT, seq), BF16),
        ],
        compiler_params=_params(("arbitrary", "arbitrary")),
        name="mla_projections",
    )(y, y, y, gq, gkv, cos, sin, cos.T, sin.T, wqt, wk, wvt)


def _flash_store(o_ref, acc):
    dv = o_ref.shape[2]
    o_t = acc[:dv] * (1.0 / acc[dv:dv + 1])
    o_ref[0] = o_t.T.astype(o_ref.dtype)


def _flash_kernel(qt_ref, k_ref, vt_ref, o_ref, m_ref, acc_ref, *, bkv, n_kv, first):
    qt = qt_ref[0, 0]
    seq = k_ref.shape[2]

    def scores(lo, hi):
        return jnp.dot(k_ref[0, 0, lo:hi, :], qt, preferred_element_type=F32)

    def weighted(lo, hi, pt):
        return jnp.dot(vt_ref[0, 0, :, lo:hi], pt.astype(BF16), preferred_element_type=F32)

    st = scores(0, first)
    m0 = jnp.max(st, axis=0, keepdims=True)
    acc = weighted(0, first, jnp.exp2(st - m0))
    seen = m0
    for lo in range(first, seq, bkv):
        hi = min(lo + bkv, seq)
        st = scores(lo, hi)
        seen = jnp.maximum(seen, jnp.max(st, axis=0, keepdims=True))
        acc = acc + weighted(lo, hi, jnp.exp2(st - m0))
    _flash_store(o_ref, acc)

    @pl.when(jnp.max(seen - m0) > FLASH_MAX_EXCESS)
    def _():
        m_ref[...] = jnp.full(m_ref.shape, -jnp.inf, F32)
        acc_ref[...] = jnp.zeros(acc_ref.shape, F32)

        def body(c, carry):
            start = pl.multiple_of(c * bkv, bkv)
            s = jnp.dot(k_ref[0, 0, pl.ds(start, bkv), :], qt, preferred_element_type=F32)
            m_old = m_ref[...]
            m_new = jnp.maximum(m_old, jnp.max(s, axis=0, keepdims=True))
            pv = jnp.dot(vt_ref[0, 0, :, pl.ds(start, bkv)], jnp.exp2(s - m_new).astype(BF16),
                         preferred_element_type=F32)
            acc_ref[...] = jnp.exp2(m_old - m_new) * acc_ref[...] + pv
            m_ref[...] = m_new
            return carry

        lax.fori_loop(0, n_kv, body, 0)
        _flash_store(o_ref, acc_ref[...])


def flash_attention(qt, k, vt, *, bq, bkv, first):
    batch, heads, dq, seq = qt.shape
    dv_ext = vt.shape[2]
    return pl.pallas_call(
        functools.partial(_flash_kernel, bkv=bkv, n_kv=seq // bkv, first=first),
        grid=(batch, heads, seq // bq),
        in_specs=[
            pl.BlockSpec((1, 1, dq, bq), lambda b, h, i: (b, h, 0, i)),
            pl.BlockSpec((1, 1, seq, dq), lambda b, h, i: (b, h, 0, 0)),
            pl.BlockSpec((1, 1, dv_ext, seq), lambda b, h, i: (b, h, 0, 0)),
        ],
        out_specs=pl.BlockSpec((1, bq, V_HEAD), lambda b, h, i: (b, i, h)),
        out_shape=jax.ShapeDtypeStruct((batch, seq, heads * V_HEAD), BF16),
        scratch_shapes=[pltpu.VMEM((1, bq), F32), pltpu.VMEM((dv_ext, bq), F32)],
        compiler_params=_params(("arbitrary", "arbitrary", "arbitrary")),
        name="flash_attention",
    )(qt, k, vt)


def _prep_na_w_in(w):
    q, k, v, z = jnp.split(w, 4, axis=1)
    q = q * (NA_HEAD_DIM ** -0.5 * LOG2E)
    return jnp.concatenate([z, k], axis=1).astype(BF16), jnp.concatenate([q, v], axis=1).T.astype(BF16)


def _prep_mla_w_in(w):
    cq = w[:, :Q_LORA]
    ckv = w[:, Q_LORA:Q_LORA + KV_LORA]
    kr = w[:, Q_LORA + KV_LORA:Q_LORA + KV_LORA + QK_ROPE]
    z = w[:, Q_LORA + KV_LORA + QK_ROPE:]
    kr_e, kr_o = kr[:, 0::2], kr[:, 1::2]
    pad = jnp.zeros((w.shape[0], MLA_IN_PAD - MLA_WIDTH - Q_LORA - KV_LORA - 2 * QK_ROPE), w.dtype)
    return jnp.concatenate([z, cq, ckv, kr_e, kr_o, kr_o, kr_e, pad], axis=1).astype(BF16)


def _prep_mla_w_q(w):
    w = w.reshape(Q_LORA, MLA_HEADS, QK_HEAD) * (QK_HEAD ** -0.5 * LOG2E)
    nope, rope = w[..., :QK_NOPE], w[..., QK_NOPE:]
    r_e, r_o = rope[..., 0::2], rope[..., 1::2]
    ext = jnp.concatenate([nope, r_e, r_o, r_o, r_e], axis=-1)
    return ext.reshape(Q_LORA, MLA_HEADS * MLA_QK_PAD).astype(BF16)


def _prep_mla_w_kv(w):
    w = w.reshape(KV_LORA, MLA_HEADS, QK_NOPE + V_HEAD)
    wk = w[..., :QK_NOPE].reshape(KV_LORA, MLA_HEADS * QK_NOPE)
    wvt = w[..., QK_NOPE:].reshape(KV_LORA, MLA_HEADS * V_HEAD).T
    return wk.astype(BF16), wvt.astype(BF16)


def _rope_tables(length):
    inv_freq = 1.0 / (ROPE_THETA ** (jnp.arange(0, QK_ROPE, 2, dtype=F32) / QK_ROPE))
    ang = jnp.arange(length, dtype=F32)[:, None] * inv_freq[None, :]
    c, s = jnp.cos(ang), jnp.sin(ang)
    zeros = jnp.zeros((length, QK_ROPE), F32)
    return jnp.concatenate([c, c, zeros], axis=1), jnp.concatenate([-s, s, zeros], axis=1)


def kernel(x, norm_pre, norm_post, na_w_in, na_rpb, na_w_out, mla_w_in, mla_q_norm,
           mla_w_q_b, mla_kv_norm, mla_w_kv_b, mla_w_out):
    batch, seq, d = x.shape
    xt = x.reshape(batch * seq, d)
    cos_t, sin_t = _rope_tables(seq)
    for i in range(DEPTH):
        j = i // 2
        g_pre = norm_pre[i].reshape(1, d)
        g_post = norm_post[i].reshape(1, d)
        if i % 2 == 0:
            w_zk, wt_qv = _prep_na_w_in(na_w_in[j])
            y = norm_matmul(xt, g_pre, w_zk, bm=1024, bn=1024)
            qvt = norm_matmul_t(xt, g_pre, wt_qv, bm=1024, bn=1024)
            table = na_bias_table(na_rpb[j])
            o = na_attention(y, qvt, table, batch=batch, seq=seq, groups_per_step=4)
            w_out = na_w_out[j]
        else:
            y = norm_matmul(xt, g_pre, _prep_mla_w_in(mla_w_in[j]), bm=512, bn=MLA_IN_PAD // 2)
            wk, wvt = _prep_mla_w_kv(mla_w_kv_b[j])
            qt, k, vt = mla_projections(
                y, mla_q_norm[j].reshape(1, -1), mla_kv_norm[j].reshape(1, -1), cos_t, sin_t,
                _prep_mla_w_q(mla_w_q_b[j]).T, wk, wvt, batch=batch, seq=seq, bm=256)
            o = flash_attention(qt, k, vt, bq=1024, bkv=1024, first=256).reshape(batch * seq, MLA_WIDTH)
            w_out = mla_w_out[j]
        xt = gate_out(o, y, w_out.astype(BF16), xt, g_post, bm=256)
    return xt.reshape(batch, seq, d)
```

```python
import functools
import math

import jax
import jax.numpy as jnp
from jax import lax
from jax.experimental import pallas as pl
from jax.experimental.pallas import tpu as pltpu

F32 = jnp.float32
BF16 = jnp.bfloat16

D_MODEL = 2048
DEPTH = 4
GRID_W = 64
RMS_EPS = 1e-6
LOG2E = math.log2(math.e)
NEG = -1e30

NA_HEADS = 16
NA_HEAD_DIM = 128
NA_WIDTH = NA_HEADS * NA_HEAD_DIM
NA_Q_SCALE = NA_HEAD_DIM ** -0.5 * LOG2E
NA_KH = 8
NA_KW = 16
NA_BIAS_ROWS = 2 * NA_KH - 1
NA_BIAS_COLS = 2 * NA_KW - 1
NA_TABLE_SLABS = NA_BIAS_ROWS + 1
NA_ROW_GROUP = 8
NA_KEY_ROWS = 16
NA_Q_TILE = NA_ROW_GROUP * GRID_W
NA_K_TILE = NA_KEY_ROWS * GRID_W
NA_SAFE_ROWS = (6, 10)
NA_MAX_EXCESS = 32.0

MLA_HEADS = 16
Q_LORA = 512
KV_LORA = 512
QK_NOPE = 128
QK_ROPE = 64
QK_HEAD = QK_NOPE + QK_ROPE
V_HEAD = 128
MLA_WIDTH = MLA_HEADS * V_HEAD
ROPE_THETA = 10000.0
MLA_QK_PAD = 256
MLA_V_EXT = V_HEAD + 16
FLASH_MAX_EXCESS = 32.0
MLA_IN_PAD = MLA_WIDTH + Q_LORA + KV_LORA + 256

VMEM_LIMIT_V7X = 56 * 1024 * 1024


def _params(semantics):
    return pltpu.CompilerParams(dimension_semantics=semantics, vmem_limit_bytes=VMEM_LIMIT_V7X)


def _rms(x, g):
    return x * lax.rsqrt(jnp.mean(x * x, axis=-1, keepdims=True) + RMS_EPS) * g


def _norm_matmul_kernel(x_ref, g_ref, w_ref, o_ref, h_ref):
    @pl.when(pl.program_id(1) == 0)
    def _():
        h_ref[...] = _rms(x_ref[...], g_ref[...]).astype(BF16)

    o_ref[...] = jnp.dot(h_ref[...], w_ref[...], preferred_element_type=F32).astype(o_ref.dtype)


def norm_matmul(x, g, w, *, bm, bn):
    t, d = x.shape
    n = w.shape[1]
    return pl.pallas_call(
        _norm_matmul_kernel,
        grid=(t // bm, n // bn),
        in_specs=[
            pl.BlockSpec((bm, d), lambda i, j: (i, 0)),
            pl.BlockSpec((1, d), lambda i, j: (0, 0)),
            pl.BlockSpec((d, bn), lambda i, j: (0, j)),
        ],
        out_specs=pl.BlockSpec((bm, bn), lambda i, j: (i, j)),
        out_shape=jax.ShapeDtypeStruct((t, n), BF16),
        scratch_shapes=[pltpu.VMEM((bm, d), BF16)],
        compiler_params=_params(("arbitrary", "arbitrary")),
        name="norm_matmul",
    )(x, g, w)


def _na_proj_kernel(x_ref, g_ref, w_ref, o_ref, h_ref, *, transposed, scaled_blocks):
    j = pl.program_id(1)

    @pl.when(j == 0)
    def _():
        h_ref[...] = _rms(x_ref[...], g_ref[...]).astype(BF16)

    w = w_ref[0].astype(BF16)
    if transposed:
        res = lax.dot_general(w, h_ref[...], (((0,), (1,)), ((), ())), preferred_element_type=F32)
    else:
        res = jnp.dot(h_ref[...], w, preferred_element_type=F32)
    if scaled_blocks:
        res = res * jnp.where(j < scaled_blocks, NA_Q_SCALE, 1.0)
    o_ref[...] = res.astype(o_ref.dtype)


def na_projection(x, g, w_all, layer, *, sections, transposed, scale_first, bm, bn):
    t, d = x.shape
    nb = NA_WIDTH // bn
    sec_a, sec_b = sections

    def w_block(i, j):
        return layer, 0, jnp.where(j < nb, sec_a * nb + j, sec_b * nb + j - nb)

    n = 2 * NA_WIDTH
    if transposed:
        out_spec = pl.BlockSpec((bn, bm), lambda i, j: (j, i))
        out_shape = jax.ShapeDtypeStruct((n, t), BF16)
    else:
        out_spec = pl.BlockSpec((bm, bn), lambda i, j: (i, j))
        out_shape = jax.ShapeDtypeStruct((t, n), BF16)
    return pl.pallas_call(
        functools.partial(_na_proj_kernel, transposed=transposed, scaled_blocks=nb if scale_first else 0),
        grid=(t // bm, n // bn),
        in_specs=[
            pl.BlockSpec((bm, d), lambda i, j: (i, 0)),
            pl.BlockSpec((1, d), lambda i, j: (0, 0)),
            pl.BlockSpec((1, d, bn), w_block),
        ],
        out_specs=out_spec,
        out_shape=out_shape,
        scratch_shapes=[pltpu.VMEM((bm, d), BF16)],
        compiler_params=_params(("arbitrary", "arbitrary")),
        name="na_projection_t" if transposed else "na_projection",
    )(x, g, w_all)


def _gate_out_kernel(o_ref, z_ref, w_ref, x_ref, g_ref, out_ref, *, n_chains):
    rows = o_ref.shape[0] // n_chains
    for c in range(n_chains):
        sl = slice(c * rows, (c + 1) * rows)
        z = z_ref[sl, :].astype(F32)
        gate = z * (1.0 / (1.0 + jnp.exp(-z)))
        a = (o_ref[sl, :].astype(F32) * gate).astype(BF16)
        y = jnp.dot(a, w_ref[...], preferred_element_type=F32)
        out_ref[sl, :] = x_ref[sl, :] + _rms(y, g_ref[...])


def gate_out(o, yz, w_out, x, g, *, bm, n_chains):
    t, d = x.shape
    return pl.pallas_call(
        functools.partial(_gate_out_kernel, n_chains=n_chains),
        grid=(t // bm,),
        in_specs=[
            pl.BlockSpec((bm, d), lambda i: (i, 0)),
            pl.BlockSpec((bm, d), lambda i: (i, 0)),
            pl.BlockSpec((d, d), lambda i: (0, 0)),
            pl.BlockSpec((bm, d), lambda i: (i, 0)),
            pl.BlockSpec((1, d), lambda i: (0, 0)),
        ],
        out_specs=pl.BlockSpec((bm, d), lambda i: (i, 0)),
        out_shape=jax.ShapeDtypeStruct((t, d), F32),
        compiler_params=_params(("arbitrary",)),
        name="gate_out",
    )(o, yz, w_out, x, g)


def _na_bias_table_kernel(rpb_ref, out_ref):
    h = pl.program_id(0)
    kc = lax.broadcasted_iota(jnp.int32, (GRID_W, 2 * GRID_W), 0)
    lane = lax.broadcasted_iota(jnp.int32, (GRID_W, 2 * GRID_W), 1)
    c = lane & (GRID_W - 1)
    diff = kc - c + (NA_KW - 1)
    col_start = jnp.clip(c - NA_KW // 2, 0, GRID_W - NA_KW)
    valid = (kc >= col_start) & (kc < col_start + NA_KW)
    neg = jnp.full((GRID_W, 2 * GRID_W), NEG, F32)

    def toeplitz(dr):
        base = (h * NA_BIAS_ROWS + dr) * NA_BIAS_COLS

        acc = jnp.zeros((GRID_W, 2 * GRID_W), F32)
        for d in range(NA_BIAS_COLS):
            acc = jnp.where(diff == d, rpb_ref[base + d], acc)
        return jnp.where(valid, acc * LOG2E, neg)

    slabs = [toeplitz(dr) for dr in range(NA_BIAS_ROWS)]

    def slab(dr):
        return slabs[dr] if 0 <= dr < NA_BIAS_ROWS else neg

    for t in range(NA_TABLE_SLABS):
        out_ref[0, t] = jnp.where(lane >= GRID_W, slab(t - 1), slab(t))


def na_bias_table(rpb):
    heads = rpb.shape[0]
    return pl.pallas_call(
        _na_bias_table_kernel,
        grid=(heads,),
        in_specs=[pl.BlockSpec(memory_space=pltpu.SMEM)],
        out_specs=pl.BlockSpec((1, NA_TABLE_SLABS, GRID_W, 2 * GRID_W), lambda h: (h, 0, 0, 0)),
        out_shape=jax.ShapeDtypeStruct((heads, NA_TABLE_SLABS, GRID_W, 2 * GRID_W), F32),
        compiler_params=_params(("arbitrary",)),
        name="na_bias_table",
    )(rpb.reshape(-1))


def _na_group_geometry(kind, rows):
    if kind == 0:
        return 0, 0
    if kind == 1:
        return NA_ROW_GROUP, NA_ROW_GROUP - NA_KH // 2
    return rows - NA_ROW_GROUP, rows - NA_KEY_ROWS


def _build_na_bias_tile(table_ref, bias_ref, kind, rows):
    r0, kr0 = _na_group_geometry(kind, rows)
    lane = lax.broadcasted_iota(jnp.int32, (GRID_W, 2 * GRID_W), 1)
    neg = jnp.full((GRID_W, 2 * GRID_W), NEG, F32)

    def in_window(r, kr):
        row_start = min(max(r - NA_KH // 2, 0), rows - NA_KH)
        return row_start <= kr < row_start + NA_KH

    for i in range(NA_ROW_GROUP):
        assert any(in_window(r0 + i, kr0 + j) for j in range(*NA_SAFE_ROWS))
    for j in range(NA_KEY_ROWS):
        kr = kr0 + j
        for ip in range(NA_ROW_GROUP // 2):
            r_l = r0 + 2 * ip
            ok_l = in_window(r_l, kr)
            ok_r = in_window(r_l + 1, kr)
            t = kr - r_l + (NA_KH - 1)
            if ok_l or ok_r:
                tile = table_ref[0, t]
                if not ok_l:
                    tile = jnp.where(lane >= GRID_W, tile, neg)
                elif not ok_r:
                    tile = jnp.where(lane >= GRID_W, neg, tile)
            else:
                tile = neg
            bias_ref[kind, j * GRID_W:(j + 1) * GRID_W, ip * 2 * GRID_W:(ip + 1) * 2 * GRID_W] = tile


def _na_attn_kernel(qt_ref, k_ref, vt_ref, table_ref, o_ref, bias_ref, *, rows, groups_per_step):
    b = pl.program_id(1)
    step = pl.program_id(2)
    n_groups = rows // NA_ROW_GROUP

    @pl.when((b == 0) & (step == 0))
    def _():
        for kind in range(3):
            _build_na_bias_tile(table_ref, bias_ref, kind, rows)

    safe_lo, safe_hi = NA_SAFE_ROWS[0] * GRID_W, NA_SAFE_ROWS[1] * GRID_W
    ones = jnp.ones((16, NA_K_TILE), BF16)

    def group(u):
        g = step * groups_per_step + u
        kr0 = jnp.clip(g * NA_ROW_GROUP - NA_KH // 2, 0, rows - NA_KEY_ROWS)
        start = pl.multiple_of(kr0 * GRID_W, NA_KH // 2 * GRID_W)
        kind = jnp.where(g == 0, 0, jnp.where(g == n_groups - 1, 2, 1))
        qt = qt_ref[:, u * NA_Q_TILE:(u + 1) * NA_Q_TILE]
        k = k_ref[pl.ds(start, NA_K_TILE), :]
        vt_ext = jnp.concatenate([vt_ref[:, pl.ds(start, NA_K_TILE)], ones], axis=0)

        def scores(lo, hi):
            return jnp.dot(k[lo:hi], qt, preferred_element_type=F32) + bias_ref[kind, lo:hi, :]

        def weighted(lo, hi, pt):
            return jnp.dot(vt_ext[:, lo:hi], pt.astype(BF16), preferred_element_type=F32)

        def store(acc):
            o_t = acc[:NA_HEAD_DIM] * (1.0 / acc[NA_HEAD_DIM:NA_HEAD_DIM + 1])
            o_ref[u * NA_Q_TILE:(u + 1) * NA_Q_TILE, :] = o_t.T.astype(o_ref.dtype)

        return scores, weighted, store

    excess = []
    for u in range(groups_per_step):
        scores, weighted, store = group(u)
        st = scores(safe_lo, safe_hi)
        m0 = jnp.max(st, axis=0, keepdims=True)
        acc = weighted(safe_lo, safe_hi, jnp.exp2(st - m0))
        seen = m0
        for lo, hi in ((0, safe_lo), (safe_hi, NA_K_TILE)):
            st = scores(lo, hi)
            seen = jnp.maximum(seen, jnp.max(st, axis=0, keepdims=True))
            acc = acc + weighted(lo, hi, jnp.exp2(st - m0))
        store(acc)
        excess.append(seen - m0)

    @pl.when(jnp.max(jnp.concatenate(excess, axis=0)) > NA_MAX_EXCESS)
    def _():
        for u in range(groups_per_step):
            scores, weighted, store = group(u)
            s_all = scores(0, NA_K_TILE)
            m = jnp.max(s_all, axis=0, keepdims=True)
            store(weighted(0, NA_K_TILE, jnp.exp2(s_all - m)))


def na_attention(zk, qvt, table, *, batch, seq, groups_per_step):
    rows = seq // GRID_W
    n_steps = rows // NA_ROW_GROUP // groups_per_step
    bq = NA_Q_TILE * groups_per_step
    hb = NA_WIDTH // NA_HEAD_DIM
    return pl.pallas_call(
        functools.partial(_na_attn_kernel, rows=rows, groups_per_step=groups_per_step),
        grid=(NA_HEADS, batch, n_steps),
        in_specs=[
            pl.BlockSpec((NA_HEAD_DIM, bq), lambda h, b, s: (h, b * n_steps + s)),
            pl.BlockSpec((seq, NA_HEAD_DIM), lambda h, b, s: (b, hb + h)),
            pl.BlockSpec((NA_HEAD_DIM, seq), lambda h, b, s: (hb + h, b)),
            pl.BlockSpec((1, NA_TABLE_SLABS, GRID_W, 2 * GRID_W), lambda h, b, s: (h, 0, 0, 0)),
        ],
        out_specs=pl.BlockSpec((bq, NA_HEAD_DIM), lambda h, b, s: (b * n_steps + s, h)),
        out_shape=jax.ShapeDtypeStruct((batch * seq, NA_WIDTH), BF16),
        scratch_shapes=[pltpu.VMEM((3, NA_K_TILE, NA_Q_TILE), F32)],
        compiler_params=_params(("arbitrary", "arbitrary", "arbitrary")),
        name="na_attention",
    )(qvt, zk, qvt, table)


def _mla_proj_kernel(cq_ref, ckv_ref, kr_ref, gq_ref, gkv_ref, cos_ref, sin_ref, cos_t_ref, sin_t_ref,
                     wqt_ref, wk_ref, wvt_ref, qt_ref, k_ref, vt_ref):
    hq = _rms(cq_ref[...].astype(F32), gq_ref[...]).astype(BF16)
    hkv = _rms(ckv_ref[...].astype(F32), gkv_ref[...]).astype(BF16)
    kr = kr_ref[...].astype(F32)
    k_roped = (kr * cos_ref[...] + pltpu.roll(kr, QK_ROPE, 1) * sin_ref[...]).astype(BF16)
    cos_t = cos_t_ref[...]
    sin_t = sin_t_ref[...]
    nt = (((1,), (1,)), ((), ()))
    qt_all = lax.dot_general(wqt_ref[...], hq, nt, preferred_element_type=F32)
    kn_all = jnp.dot(hkv, wk_ref[...], preferred_element_type=F32)
    vt_all = lax.dot_general(wvt_ref[...], hkv, nt, preferred_element_type=F32)
    ones = jnp.ones((MLA_V_EXT - V_HEAD, hq.shape[0]), BF16)
    for h in range(MLA_HEADS):
        r0 = h * MLA_QK_PAD
        qt_ref[0, h, :QK_NOPE, :] = qt_all[r0:r0 + QK_NOPE].astype(BF16)
        y2 = qt_all[r0 + QK_NOPE:r0 + MLA_QK_PAD]
        partner = jnp.concatenate([y2[QK_ROPE:], y2[:QK_ROPE]], axis=0)
        qt_ref[0, h, QK_NOPE:, :] = (y2 * cos_t + partner * sin_t).astype(BF16)
        k_ref[0, h, :, :QK_NOPE] = kn_all[:, h * QK_NOPE:(h + 1) * QK_NOPE].astype(BF16)
        k_ref[0, h, :, QK_NOPE:] = k_roped
        vt_ref[0, h, :V_HEAD, :] = vt_all[h * V_HEAD:(h + 1) * V_HEAD, :].astype(BF16)
        vt_ref[0, h, V_HEAD:, :] = ones


def mla_projections(y, gq, gkv, cos, sin, wqt, wk, wvt, *, batch, seq, bm):
    nb = seq // bm
    cq_blk = MLA_WIDTH // Q_LORA
    kr_blk = (MLA_WIDTH + Q_LORA + KV_LORA) // 128
    const2 = lambda b, i: (0, 0)
    return pl.pallas_call(
        _mla_proj_kernel,
        grid=(batch, nb),
        in_specs=[
            pl.BlockSpec((bm, Q_LORA), lambda b, i: (b * nb + i, cq_blk)),
            pl.BlockSpec((bm, KV_LORA), lambda b, i: (b * nb + i, cq_blk + 1)),
            pl.BlockSpec((bm, 128), lambda b, i: (b * nb + i, kr_blk)),
            pl.BlockSpec((1, Q_LORA), const2),
            pl.BlockSpec((1, KV_LORA), const2),
            pl.BlockSpec((bm, 128), lambda b, i: (i, 0)),
            pl.BlockSpec((bm, 128), lambda b, i: (i, 0)),
            pl.BlockSpec((128, bm), lambda b, i: (0, i)),
            pl.BlockSpec((128, bm), lambda b, i: (0, i)),
            pl.BlockSpec(wqt.shape, const2),
            pl.BlockSpec(wk.shape, const2),
            pl.BlockSpec(wvt.shape, const2),
        ],
        out_specs=[
            pl.BlockSpec((1, MLA_HEADS, MLA_QK_PAD, bm), lambda b, i: (b, 0, 0, i)),
            pl.BlockSpec((1, MLA_HEADS, bm, MLA_QK_PAD), lambda b, i: (b, 0, i, 0)),
            pl.BlockSpec((1, MLA_HEADS, MLA_V_EXT, bm), lambda b, i: (b, 0, 0, i)),
        ],
        out_shape=[
            jax.ShapeDtypeStruct((batch, MLA_HEADS, MLA_QK_PAD, seq), BF16),
            jax.ShapeDtypeStruct((batch, MLA_HEADS, seq, MLA_QK_PAD), BF16),
            jax.ShapeDtypeStruct((batch, MLA_HEADS, MLA_V_EXT, seq), BF16),
        ],
        compiler_params=_params(("arbitrary", "arbitrary")),
        name="mla_projections",
    )(y, y, y, gq, gkv, cos, sin, cos.T, sin.T, wqt, wk, wvt)


def _flash_store(o_ref, acc):
    dv = o_ref.shape[2]
    o_t = acc[:dv] * (1.0 / acc[dv:dv + 1])
    o_ref[0] = o_t.T.astype(o_ref.dtype)


def _flash_kernel(qt_ref, k_ref, vt_ref, o_ref, m_ref, acc_ref, *, bkv, n_kv, first):
    qt = qt_ref[0, 0]
    seq = k_ref.shape[2]

    def scores(lo, hi):
        return jnp.dot(k_ref[0, 0, lo:hi, :], qt, preferred_element_type=F32)

    def weighted(lo, hi, pt):
        return jnp.dot(vt_ref[0, 0, :, lo:hi], pt.astype(BF16), preferred_element_type=F32)

    st = scores(0, first)
    m0 = jnp.max(st, axis=0, keepdims=True)
    acc = weighted(0, first, jnp.exp2(st - m0))
    seen = m0
    for lo in range(first, seq, bkv):
        hi = min(lo + bkv, seq)
        st = scores(lo, hi)
        seen = jnp.maximum(seen, jnp.max(st, axis=0, keepdims=True))
        acc = acc + weighted(lo, hi, jnp.exp2(st - m0))
    _flash_store(o_ref, acc)

    @pl.when(jnp.max(seen - m0) > FLASH_MAX_EXCESS)
    def _():
        m_ref[...] = jnp.full(m_ref.shape, -jnp.inf, F32)
        acc_ref[...] = jnp.zeros(acc_ref.shape, F32)

        def body(c, carry):
            start = pl.multiple_of(c * bkv, bkv)
            s = jnp.dot(k_ref[0, 0, pl.ds(start, bkv), :], qt, preferred_element_type=F32)
            m_old = m_ref[...]
            m_new = jnp.maximum(m_old, jnp.max(s, axis=0, keepdims=True))
            pv = jnp.dot(vt_ref[0, 0, :, pl.ds(start, bkv)], jnp.exp2(s - m_new).astype(BF16),
                         preferred_element_type=F32)
            acc_ref[...] = jnp.exp2(m_old - m_new) * acc_ref[...] + pv
            m_ref[...] = m_new
            return carry

        lax.fori_loop(0, n_kv, body, 0)
        _flash_store(o_ref, acc_ref[...])


def flash_attention(qt, k, vt, *, bq, bkv, first):
    batch, heads, dq, seq = qt.shape
    dv_ext = vt.shape[2]
    return pl.pallas_call(
        functools.partial(_flash_kernel, bkv=bkv, n_kv=seq // bkv, first=first),
        grid=(batch, heads, seq // bq),
        in_specs=[
            pl.BlockSpec((1, 1, dq, bq), lambda b, h, i: (b, h, 0, i)),
            pl.BlockSpec((1, 1, seq, dq), lambda b, h, i: (b, h, 0, 0)),
            pl.BlockSpec((1, 1, dv_ext, seq), lambda b, h, i: (b, h, 0, 0)),
        ],
        out_specs=pl.BlockSpec((1, bq, V_HEAD), lambda b, h, i: (b, i, h)),
        out_shape=jax.ShapeDtypeStruct((batch, seq, heads * V_HEAD), BF16),
        scratch_shapes=[pltpu.VMEM((1, bq), F32), pltpu.VMEM((dv_ext, bq), F32)],
        compiler_params=_params(("arbitrary", "arbitrary", "arbitrary")),
        name="flash_attention",
    )(qt, k, vt)


def _prep_mla_w_in(w):
    cq = w[:, :Q_LORA]
    ckv = w[:, Q_LORA:Q_LORA + KV_LORA]
    kr = w[:, Q_LORA + KV_LORA:Q_LORA + KV_LORA + QK_ROPE]
    z = w[:, Q_LORA + KV_LORA + QK_ROPE:]
    kr_e, kr_o = kr[:, 0::2], kr[:, 1::2]
    pad = jnp.zeros((w.shape[0], MLA_IN_PAD - MLA_WIDTH - Q_LORA - KV_LORA - 2 * QK_ROPE), w.dtype)
    return jnp.concatenate([z, cq, ckv, kr_e, kr_o, kr_o, kr_e, pad], axis=1).astype(BF16)


def _prep_mla_w_q(w):
    w = w.reshape(Q_LORA, MLA_HEADS, QK_HEAD) * (QK_HEAD ** -0.5 * LOG2E)
    nope, rope = w[..., :QK_NOPE], w[..., QK_NOPE:]
    r_e, r_o = rope[..., 0::2], rope[..., 1::2]
    ext = jnp.concatenate([nope, r_e, r_o, r_o, r_e], axis=-1)
    return ext.reshape(Q_LORA, MLA_HEADS * MLA_QK_PAD).astype(BF16)


def _prep_mla_w_kv(w):
    w = w.reshape(KV_LORA, MLA_HEADS, QK_NOPE + V_HEAD)
    wk = w[..., :QK_NOPE].reshape(KV_LORA, MLA_HEADS * QK_NOPE)
    wvt = w[..., QK_NOPE:].reshape(KV_LORA, MLA_HEADS * V_HEAD).T
    return wk.astype(BF16), wvt.astype(BF16)


def _rope_tables(length):
    inv_freq = 1.0 / (ROPE_THETA ** (jnp.arange(0, QK_ROPE, 2, dtype=F32) / QK_ROPE))
    ang = jnp.arange(length, dtype=F32)[:, None] * inv_freq[None, :]
    c, s = jnp.cos(ang), jnp.sin(ang)
    zeros = jnp.zeros((length, QK_ROPE), F32)
    return jnp.concatenate([c, c, zeros], axis=1), jnp.concatenate([-s, s, zeros], axis=1)


def kernel(x, norm_pre, norm_post, na_w_in, na_rpb, na_w_out, mla_w_in, mla_q_norm,
           mla_w_q_b, mla_kv_norm, mla_w_kv_b, mla_w_out):
    batch, seq, d = x.shape
    xt = x.reshape(batch * seq, d)
    cos_t, sin_t = _rope_tables(seq)
    for i in range(DEPTH):
        j = i // 2
        g_pre = norm_pre[i].reshape(1, d)
        g_post = norm_post[i].reshape(1, d)
        if i % 2 == 0:
            y = na_projection(xt, g_pre, na_w_in, j, sections=(3, 1), transposed=False,
                              scale_first=False, bm=1024, bn=1024)
            qvt = na_projection(xt, g_pre, na_w_in, j, sections=(0, 2), transposed=True,
                                scale_first=True, bm=1024, bn=1024)
            table = na_bias_table(na_rpb[j])
            o = na_attention(y, qvt, table, batch=batch, seq=seq, groups_per_step=4)
            w_out = na_w_out[j]
        else:
            y = norm_matmul(xt, g_pre, _prep_mla_w_in(mla_w_in[j]), bm=1024, bn=MLA_IN_PAD // 2)
            wk, wvt = _prep_mla_w_kv(mla_w_kv_b[j])
            qt, k, vt = mla_projections(
                y, mla_q_norm[j].reshape(1, -1), mla_kv_norm[j].reshape(1, -1), cos_t, sin_t,
                _prep_mla_w_q(mla_w_q_b[j]).T, wk, wvt, batch=batch, seq=seq, bm=256)
            o = flash_attention(qt, k, vt, bq=1024, bkv=1024, first=256).reshape(batch * seq, MLA_WIDTH)
            w_out = mla_w_out[j]
        xt = gate_out(o, y, w_out.astype(BF16), xt, g_post, bm=512, n_chains=2)
    return xt.reshape(batch, seq, d)
```

```python
import functools
import math

import jax
import jax.numpy as jnp
from jax import lax
from jax.experimental import pallas as pl
from jax.experimental.pallas import tpu as pltpu

F32 = jnp.float32
BF16 = jnp.bfloat16

D_MODEL = 2048
DEPTH = 4
GRID_W = 64
RMS_EPS = 1e-6
LOG2E = math.log2(math.e)
NEG = -1e30

NA_HEADS = 16
NA_HEAD_DIM = 128
NA_WIDTH = NA_HEADS * NA_HEAD_DIM
NA_Q_SCALE = NA_HEAD_DIM ** -0.5 * LOG2E
NA_KH = 8
NA_KW = 16
NA_BIAS_ROWS = 2 * NA_KH - 1
NA_BIAS_COLS = 2 * NA_KW - 1
NA_TABLE_SLABS = NA_BIAS_ROWS + 1
NA_ROW_GROUP = 8
NA_KEY_ROWS = 16
NA_Q_TILE = NA_ROW_GROUP * GRID_W
NA_K_TILE = NA_KEY_ROWS * GRID_W
NA_SAFE_ROWS = (6, 10)
NA_MAX_EXCESS = 32.0

MLA_HEADS = 16
Q_LORA = 512
KV_LORA = 512
QK_NOPE = 128
QK_ROPE = 64
QK_HEAD = QK_NOPE + QK_ROPE
V_HEAD = 128
MLA_WIDTH = MLA_HEADS * V_HEAD
ROPE_THETA = 10000.0
MLA_QK_PAD = 256
MLA_V_EXT = V_HEAD + 16
FLASH_MAX_EXCESS = 32.0
MLA_IN_PAD = MLA_WIDTH + Q_LORA + KV_LORA + 256

VMEM_LIMIT_V7X = 56 * 1024 * 1024


def _params(semantics):
    return pltpu.CompilerParams(dimension_semantics=semantics, vmem_limit_bytes=VMEM_LIMIT_V7X)


def _rms(x, g):
    return x * lax.rsqrt(jnp.mean(x * x, axis=-1, keepdims=True) + RMS_EPS) * g


def _rms_norm_kernel(x_ref, g_ref, h_ref):
    h_ref[...] = _rms(x_ref[...], g_ref[...]).astype(h_ref.dtype)


def rms_norm(x, g, *, bm):
    t, d = x.shape
    return pl.pallas_call(
        _rms_norm_kernel,
        grid=(t // bm,),
        in_specs=[pl.BlockSpec((bm, d), lambda i: (i, 0)), pl.BlockSpec((1, d), lambda i: (0, 0))],
        out_specs=pl.BlockSpec((bm, d), lambda i: (i, 0)),
        out_shape=jax.ShapeDtypeStruct((t, d), BF16),
        compiler_params=_params(("arbitrary",)),
        name="rms_norm",
    )(x, g)


def _matmul_kernel(h_ref, w_ref, o_ref):
    o_ref[...] = jnp.dot(h_ref[...], w_ref[...], preferred_element_type=F32).astype(o_ref.dtype)


def matmul(h, w, *, bm, bn):
    t, d = h.shape
    n = w.shape[1]
    return pl.pallas_call(
        _matmul_kernel,
        grid=(n // bn, t // bm),
        in_specs=[pl.BlockSpec((bm, d), lambda j, i: (i, 0)), pl.BlockSpec((d, bn), lambda j, i: (0, j))],
        out_specs=pl.BlockSpec((bm, bn), lambda j, i: (i, j)),
        out_shape=jax.ShapeDtypeStruct((t, n), BF16),
        compiler_params=_params(("arbitrary", "arbitrary")),
        name="matmul",
    )(h, w)


def _na_proj_kernel(h_ref, w_ref, o_ref, wb_ref, *, transposed, scaled_blocks):
    j = pl.program_id(0)

    @pl.when(pl.program_id(1) == 0)
    def _():
        w = w_ref[0]
        wb_ref[...] = (w.T if transposed else w).astype(BF16)

    if transposed:
        res = lax.dot_general(wb_ref[...], h_ref[...], (((1,), (1,)), ((), ())), preferred_element_type=F32)
    else:
        res = jnp.dot(h_ref[...], wb_ref[...], preferred_element_type=F32)
    if scaled_blocks:
        res = res * jnp.where(j < scaled_blocks, NA_Q_SCALE, 1.0)
    o_ref[...] = res.astype(o_ref.dtype)


def na_projection(h, w_all, layer, *, sections, transposed, scale_first, bm, bn):
    t, d = h.shape
    nb = NA_WIDTH // bn
    sec_a, sec_b = sections

    def w_block(j, i):
        return layer, 0, jnp.where(j < nb, sec_a * nb + j, sec_b * nb + j - nb)

    n = 2 * NA_WIDTH
    if transposed:
        out_spec = pl.BlockSpec((bn, bm), lambda j, i: (j, i))
        out_shape = jax.ShapeDtypeStruct((n, t), BF16)
        wb_shape = (bn, d)
    else:
        out_spec = pl.BlockSpec((bm, bn), lambda j, i: (i, j))
        out_shape = jax.ShapeDtypeStruct((t, n), BF16)
        wb_shape = (d, bn)
    return pl.pallas_call(
        functools.partial(_na_proj_kernel, transposed=transposed, scaled_blocks=nb if scale_first else 0),
        grid=(n // bn, t // bm),
        in_specs=[pl.BlockSpec((bm, d), lambda j, i: (i, 0)), pl.BlockSpec((1, d, bn), w_block)],
        out_specs=out_spec,
        out_shape=out_shape,
        scratch_shapes=[pltpu.VMEM(wb_shape, BF16)],
        compiler_params=_params(("arbitrary", "arbitrary")),
        name="na_projection_t" if transposed else "na_projection",
    )(h, w_all)


def _gate_out_kernel(o_ref, z_ref, w_ref, x_ref, g_ref, *rest, n_chains, emit_next):
    if emit_next:
        g_next_ref, out_ref, h_next_ref = rest
    else:
        (out_ref,) = rest
    rows = o_ref.shape[0] // n_chains
    for c in range(n_chains):
        sl = slice(c * rows, (c + 1) * rows)
        z = z_ref[sl, :].astype(F32)
        gate = z * (1.0 / (1.0 + jnp.exp(-z)))
        a = (o_ref[sl, :].astype(F32) * gate).astype(BF16)
        y = jnp.dot(a, w_ref[...], preferred_element_type=F32)
        x_new = x_ref[sl, :] + _rms(y, g_ref[...])
        out_ref[sl, :] = x_new
        if emit_next:
            h_next_ref[sl, :] = _rms(x_new, g_next_ref[...]).astype(h_next_ref.dtype)


def gate_out(o, yz, w_out, x, g, g_next, *, bm, n_chains):
    t, d = x.shape
    emit_next = g_next is not None
    row_spec = pl.BlockSpec((bm, d), lambda i: (i, 0))
    gain_spec = pl.BlockSpec((1, d), lambda i: (0, 0))
    in_specs = [row_spec, row_spec, pl.BlockSpec((d, d), lambda i: (0, 0)), row_spec, gain_spec]
    args = [o, yz, w_out, x, g]
    out_specs = [row_spec]
    out_shape = [jax.ShapeDtypeStruct((t, d), F32)]
    if emit_next:
        in_specs.append(gain_spec)
        args.append(g_next)
        out_specs.append(row_spec)
        out_shape.append(jax.ShapeDtypeStruct((t, d), BF16))
    res = pl.pallas_call(
        functools.partial(_gate_out_kernel, n_chains=n_chains, emit_next=emit_next),
        grid=(t // bm,),
        in_specs=in_specs,
        out_specs=out_specs,
        out_shape=out_shape,
        compiler_params=_params(("arbitrary",)),
        name="gate_out",
    )(*args)
    return (res[0], res[1]) if emit_next else (res[0], None)


def _na_bias_table_kernel(rpb_ref, out_ref):
    h = pl.program_id(0)
    kc = lax.broadcasted_iota(jnp.int32, (GRID_W, 2 * GRID_W), 0)
    lane = lax.broadcasted_iota(jnp.int32, (GRID_W, 2 * GRID_W), 1)
    c = lane & (GRID_W - 1)
    diff = kc - c + (NA_KW - 1)
    col_start = jnp.clip(c - NA_KW // 2, 0, GRID_W - NA_KW)
    valid = (kc >= col_start) & (kc < col_start + NA_KW)
    neg = jnp.full((GRID_W, 2 * GRID_W), NEG, F32)

    def toeplitz(dr):
        base = (h * NA_BIAS_ROWS + dr) * NA_BIAS_COLS

        acc = jnp.zeros((GRID_W, 2 * GRID_W), F32)
        for d in range(NA_BIAS_COLS):
            acc = jnp.where(diff == d, rpb_ref[base + d], acc)
        return jnp.where(valid, acc * LOG2E, neg)

    slabs = [toeplitz(dr) for dr in range(NA_BIAS_ROWS)]

    def slab(dr):
        return slabs[dr] if 0 <= dr < NA_BIAS_ROWS else neg

    for t in range(NA_TABLE_SLABS):
        out_ref[0, t] = jnp.where(lane >= GRID_W, slab(t - 1), slab(t))


def na_bias_table(rpb):
    heads = rpb.shape[0]
    return pl.pallas_call(
        _na_bias_table_kernel,
        grid=(heads,),
        in_specs=[pl.BlockSpec(memory_space=pltpu.SMEM)],
        out_specs=pl.BlockSpec((1, NA_TABLE_SLABS, GRID_W, 2 * GRID_W), lambda h: (h, 0, 0, 0)),
        out_shape=jax.ShapeDtypeStruct((heads, NA_TABLE_SLABS, GRID_W, 2 * GRID_W), F32),
        compiler_params=_params(("arbitrary",)),
        name="na_bias_table",
    )(rpb.reshape(-1))


def _na_group_geometry(kind, rows):
    if kind == 0:
        return 0, 0
    if kind == 1:
        return NA_ROW_GROUP, NA_ROW_GROUP - NA_KH // 2
    return rows - NA_ROW_GROUP, rows - NA_KEY_ROWS


def _build_na_bias_tile(table_ref, bias_ref, kind, rows):
    r0, kr0 = _na_group_geometry(kind, rows)
    lane = lax.broadcasted_iota(jnp.int32, (GRID_W, 2 * GRID_W), 1)
    neg = jnp.full((GRID_W, 2 * GRID_W), NEG, F32)

    def in_window(r, kr):
        row_start = min(max(r - NA_KH // 2, 0), rows - NA_KH)
        return row_start <= kr < row_start + NA_KH

    for i in range(NA_ROW_GROUP):
        assert any(in_window(r0 + i, kr0 + j) for j in range(*NA_SAFE_ROWS))
    for j in range(NA_KEY_ROWS):
        kr = kr0 + j
        for ip in range(NA_ROW_GROUP // 2):
            r_l = r0 + 2 * ip
            ok_l = in_window(r_l, kr)
            ok_r = in_window(r_l + 1, kr)
            t = kr - r_l + (NA_KH - 1)
            if ok_l or ok_r:
                tile = table_ref[0, t]
                if not ok_l:
                    tile = jnp.where(lane >= GRID_W, tile, neg)
                elif not ok_r:
                    tile = jnp.where(lane >= GRID_W, neg, tile)
            else:
                tile = neg
            bias_ref[kind, j * GRID_W:(j + 1) * GRID_W, ip * 2 * GRID_W:(ip + 1) * 2 * GRID_W] = tile


def _na_attn_kernel(qt_ref, k_ref, vt_ref, table_ref, o_ref, bias_ref, *, rows, groups_per_step):
    b = pl.program_id(1)
    step = pl.program_id(2)
    n_groups = rows // NA_ROW_GROUP

    @pl.when((b == 0) & (step == 0))
    def _():
        for kind in range(3):
            _build_na_bias_tile(table_ref, bias_ref, kind, rows)

    safe_lo, safe_hi = NA_SAFE_ROWS[0] * GRID_W, NA_SAFE_ROWS[1] * GRID_W
    ones = jnp.ones((16, NA_K_TILE), BF16)

    def group(u):
        g = step * groups_per_step + u
        kr0 = jnp.clip(g * NA_ROW_GROUP - NA_KH // 2, 0, rows - NA_KEY_ROWS)
        start = pl.multiple_of(kr0 * GRID_W, NA_KH // 2 * GRID_W)
        kind = jnp.where(g == 0, 0, jnp.where(g == n_groups - 1, 2, 1))
        qt = qt_ref[:, u * NA_Q_TILE:(u + 1) * NA_Q_TILE]
        k = k_ref[pl.ds(start, NA_K_TILE), :]
        vt_ext = jnp.concatenate([vt_ref[:, pl.ds(start, NA_K_TILE)], ones], axis=0)

        def scores(lo, hi):
            return jnp.dot(k[lo:hi], qt, preferred_element_type=F32) + bias_ref[kind, lo:hi, :]

        def weighted(lo, hi, pt):
            return jnp.dot(vt_ext[:, lo:hi], pt.astype(BF16), preferred_element_type=F32)

        def store(acc):
            o_t = acc[:NA_HEAD_DIM] * (1.0 / acc[NA_HEAD_DIM:NA_HEAD_DIM + 1])
            o_ref[u * NA_Q_TILE:(u + 1) * NA_Q_TILE, :] = o_t.T.astype(o_ref.dtype)

        return scores, weighted, store

    excess = []
    for u in range(groups_per_step):
        scores, weighted, store = group(u)
        st = scores(safe_lo, safe_hi)
        m0 = jnp.max(st, axis=0, keepdims=True)
        acc = weighted(safe_lo, safe_hi, jnp.exp2(st - m0))
        seen = m0
        for lo, hi in ((0, safe_lo), (safe_hi, NA_K_TILE)):
            st = scores(lo, hi)
            seen = jnp.maximum(seen, jnp.max(st, axis=0, keepdims=True))
            acc = acc + weighted(lo, hi, jnp.exp2(st - m0))
        store(acc)
        excess.append(seen - m0)

    @pl.when(jnp.max(jnp.concatenate(excess, axis=0)) > NA_MAX_EXCESS)
    def _():
        for u in range(groups_per_step):
            scores, weighted, store = group(u)
            s_all = scores(0, NA_K_TILE)
            m = jnp.max(s_all, axis=0, keepdims=True)
            store(weighted(0, NA_K_TILE, jnp.exp2(s_all - m)))


def na_attention(zk, qvt, table, *, batch, seq, groups_per_step):
    rows = seq // GRID_W
    n_steps = rows // NA_ROW_GROUP // groups_per_step
    bq = NA_Q_TILE * groups_per_step
    hb = NA_WIDTH // NA_HEAD_DIM
    return pl.pallas_call(
        functools.partial(_na_attn_kernel, rows=rows, groups_per_step=groups_per_step),
        grid=(NA_HEADS, batch, n_steps),
        in_specs=[
            pl.BlockSpec((NA_HEAD_DIM, bq), lambda h, b, s: (h, b * n_steps + s)),
            pl.BlockSpec((seq, NA_HEAD_DIM), lambda h, b, s: (b, hb + h)),
            pl.BlockSpec((NA_HEAD_DIM, seq), lambda h, b, s: (hb + h, b)),
            pl.BlockSpec((1, NA_TABLE_SLABS, GRID_W, 2 * GRID_W), lambda h, b, s: (h, 0, 0, 0)),
        ],
        out_specs=pl.BlockSpec((bq, NA_HEAD_DIM), lambda h, b, s: (b * n_steps + s, h)),
        out_shape=jax.ShapeDtypeStruct((batch * seq, NA_WIDTH), BF16),
        scratch_shapes=[pltpu.VMEM((3, NA_K_TILE, NA_Q_TILE), F32)],
        compiler_params=_params(("arbitrary", "arbitrary", "arbitrary")),
        name="na_attention",
    )(qvt, zk, qvt, table)


def _mla_proj_kernel(cq_ref, ckv_ref, kr_ref, gq_ref, gkv_ref, cos_ref, sin_ref, cos_t_ref, sin_t_ref,
                     wqt_ref, wk_ref, wvt_ref, qt_ref, k_ref, vt_ref):
    hq = _rms(cq_ref[...].astype(F32), gq_ref[...]).astype(BF16)
    hkv = _rms(ckv_ref[...].astype(F32), gkv_ref[...]).astype(BF16)
    kr = kr_ref[...].astype(F32)
    k_roped = (kr * cos_ref[...] + pltpu.roll(kr, QK_ROPE, 1) * sin_ref[...]).astype(BF16)
    cos_t = cos_t_ref[...]
    sin_t = sin_t_ref[...]
    nt = (((1,), (1,)), ((), ()))
    qt_all = lax.dot_general(wqt_ref[...], hq, nt, preferred_element_type=F32)
    kn_all = jnp.dot(hkv, wk_ref[...], preferred_element_type=F32)
    vt_all = lax.dot_general(wvt_ref[...], hkv, nt, preferred_element_type=F32)
    ones = jnp.ones((MLA_V_EXT - V_HEAD, hq.shape[0]), BF16)
    for h in range(MLA_HEADS):
        r0 = h * MLA_QK_PAD
        qt_ref[0, h, :QK_NOPE, :] = qt_all[r0:r0 + QK_NOPE].astype(BF16)
        y2 = qt_all[r0 + QK_NOPE:r0 + MLA_QK_PAD]
        partner = jnp.concatenate([y2[QK_ROPE:], y2[:QK_ROPE]], axis=0)
        qt_ref[0, h, QK_NOPE:, :] = (y2 * cos_t + partner * sin_t).astype(BF16)
        k_ref[0, h, :, :QK_NOPE] = kn_all[:, h * QK_NOPE:(h + 1) * QK_NOPE].astype(BF16)
        k_ref[0, h, :, QK_NOPE:] = k_roped
        vt_ref[0, h, :V_HEAD, :] = vt_all[h * V_HEAD:(h + 1) * V_HEAD, :].astype(BF16)
        vt_ref[0, h, V_HEAD:, :] = ones


def mla_projections(y, gq, gkv, cos, sin, wqt, wk, wvt, *, batch, seq, bm):
    nb = seq // bm
    cq_blk = MLA_WIDTH // Q_LORA
    kr_blk = (MLA_WIDTH + Q_LORA + KV_LORA) // 128
    const2 = lambda b, i: (0, 0)
    return pl.pallas_call(
        _mla_proj_kernel,
        grid=(batch, nb),
        in_specs=[
            pl.BlockSpec((bm, Q_LORA), lambda b, i: (b * nb + i, cq_blk)),
            pl.BlockSpec((bm, KV_LORA), lambda b, i: (b * nb + i, cq_blk + 1)),
            pl.BlockSpec((bm, 128), lambda b, i: (b * nb + i, kr_blk)),
            pl.BlockSpec((1, Q_LORA), const2),
            pl.BlockSpec((1, KV_LORA), const2),
            pl.BlockSpec((bm, 128), lambda b, i: (i, 0)),
            pl.BlockSpec((bm, 128), lambda b, i: (i, 0)),
            pl.BlockSpec((128, bm), lambda b, i: (0, i)),
            pl.BlockSpec((128, bm), lambda b, i: (0, i)),
            pl.BlockSpec(wqt.shape, const2),
            pl.BlockSpec(wk.shape, const2),
            pl.BlockSpec(wvt.shape, const2),
        ],
        out_specs=[
            pl.BlockSpec((1, MLA_HEADS, MLA_QK_PAD, bm), lambda b, i: (b, 0, 0, i)),
            pl.BlockSpec((1, MLA_HEADS, bm, MLA_QK_PAD), lambda b, i: (b, 0, i, 0)),
            pl.BlockSpec((1, MLA_HEADS, MLA_V_EXT, bm), lambda b, i: (b, 0, 0, i)),
        ],
        out_shape=[
            jax.ShapeDtypeStruct((batch, MLA_HEADS, MLA_QK_PAD, seq), BF16),
            jax.ShapeDtypeStruct((batch, MLA_HEADS, seq, MLA_QK_PAD), BF16),
            jax.ShapeDtypeStruct((batch, MLA_HEADS, MLA_V_EXT, seq), BF16),
        ],
        compiler_params=_params(("arbitrary", "arbitrary")),
        name="mla_projections",
    )(y, y, y, gq, gkv, cos, sin, cos.T, sin.T, wqt, wk, wvt)


def _flash_store(o_ref, acc):
    dv = o_ref.shape[2]
    o_t = acc[:dv] * (1.0 / acc[dv:dv + 1])
    o_ref[0] = o_t.T.astype(o_ref.dtype)


def _flash_kernel(qt_ref, k_ref, vt_ref, o_ref, m_ref, acc_ref, *, bkv, n_kv, first):
    qt = qt_ref[0, 0]
    seq = k_ref.shape[2]

    def scores(lo, hi):
        return jnp.dot(k_ref[0, 0, lo:hi, :], qt, preferred_element_type=F32)

    def weighted(lo, hi, pt):
        return jnp.dot(vt_ref[0, 0, :, lo:hi], pt.astype(BF16), preferred_element_type=F32)

    st = scores(0, first)
    m0 = jnp.max(st, axis=0, keepdims=True)
    acc = weighted(0, first, jnp.exp2(st - m0))
    seen = m0
    for lo in range(first, seq, bkv):
        hi = min(lo + bkv, seq)
        st = scores(lo, hi)
        seen = jnp.maximum(seen, jnp.max(st, axis=0, keepdims=True))
        acc = acc + weighted(lo, hi, jnp.exp2(st - m0))
    _flash_store(o_ref, acc)

    @pl.when(jnp.max(seen - m0) > FLASH_MAX_EXCESS)
    def _():
        m_ref[...] = jnp.full(m_ref.shape, -jnp.inf, F32)
        acc_ref[...] = jnp.zeros(acc_ref.shape, F32)

        def body(c, carry):
            start = pl.multiple_of(c * bkv, bkv)
            s = jnp.dot(k_ref[0, 0, pl.ds(start, bkv), :], qt, preferred_element_type=F32)
            m_old = m_ref[...]
            m_new = jnp.maximum(m_old, jnp.max(s, axis=0, keepdims=True))
            pv = jnp.dot(vt_ref[0, 0, :, pl.ds(start, bkv)], jnp.exp2(s - m_new).astype(BF16),
                         preferred_element_type=F32)
            acc_ref[...] = jnp.exp2(m_old - m_new) * acc_ref[...] + pv
            m_ref[...] = m_new
            return carry

        lax.fori_loop(0, n_kv, body, 0)
        _flash_store(o_ref, acc_ref[...])


def flash_attention(qt, k, vt, *, bq, bkv, first):
    batch, heads, dq, seq = qt.shape
    dv_ext = vt.shape[2]
    return pl.pallas_call(
        functools.partial(_flash_kernel, bkv=bkv, n_kv=seq // bkv, first=first),
        grid=(batch, heads, seq // bq),
        in_specs=[
            pl.BlockSpec((1, 1, dq, bq), lambda b, h, i: (b, h, 0, i)),
            pl.BlockSpec((1, 1, seq, dq), lambda b, h, i: (b, h, 0, 0)),
            pl.BlockSpec((1, 1, dv_ext, seq), lambda b, h, i: (b, h, 0, 0)),
        ],
        out_specs=pl.BlockSpec((1, bq, V_HEAD), lambda b, h, i: (b, i, h)),
        out_shape=jax.ShapeDtypeStruct((batch, seq, heads * V_HEAD), BF16),
        scratch_shapes=[pltpu.VMEM((1, bq), F32), pltpu.VMEM((dv_ext, bq), F32)],
        compiler_params=_params(("arbitrary", "arbitrary", "arbitrary")),
        name="flash_attention",
    )(qt, k, vt)


def _prep_mla_w_in(w):
    cq = w[:, :Q_LORA]
    ckv = w[:, Q_LORA:Q_LORA + KV_LORA]
    kr = w[:, Q_LORA + KV_LORA:Q_LORA + KV_LORA + QK_ROPE]
    z = w[:, Q_LORA + KV_LORA + QK_ROPE:]
    kr_e, kr_o = kr[:, 0::2], kr[:, 1::2]
    pad = jnp.zeros((w.shape[0], MLA_IN_PAD - MLA_WIDTH - Q_LORA - KV_LORA - 2 * QK_ROPE), w.dtype)
    return jnp.concatenate([z, cq, ckv, kr_e, kr_o, kr_o, kr_e, pad], axis=1).astype(BF16)


def _prep_mla_w_q(w):
    w = w.reshape(Q_LORA, MLA_HEADS, QK_HEAD) * (QK_HEAD ** -0.5 * LOG2E)
    nope, rope = w[..., :QK_NOPE], w[..., QK_NOPE:]
    r_e, r_o = rope[..., 0::2], rope[..., 1::2]
    ext = jnp.concatenate([nope, r_e, r_o, r_o, r_e], axis=-1)
    return ext.reshape(Q_LORA, MLA_HEADS * MLA_QK_PAD).astype(BF16)


def _prep_mla_w_kv(w):
    w = w.reshape(KV_LORA, MLA_HEADS, QK_NOPE + V_HEAD)
    wk = w[..., :QK_NOPE].reshape(KV_LORA, MLA_HEADS * QK_NOPE)
    wvt = w[..., QK_NOPE:].reshape(KV_LORA, MLA_HEADS * V_HEAD).T
    return wk.astype(BF16), wvt.astype(BF16)


def _rope_tables(length):
    inv_freq = 1.0 / (ROPE_THETA ** (jnp.arange(0, QK_ROPE, 2, dtype=F32) / QK_ROPE))
    ang = jnp.arange(length, dtype=F32)[:, None] * inv_freq[None, :]
    c, s = jnp.cos(ang), jnp.sin(ang)
    zeros = jnp.zeros((length, QK_ROPE), F32)
    return jnp.concatenate([c, c, zeros], axis=1), jnp.concatenate([-s, s, zeros], axis=1)


def kernel(x, norm_pre, norm_post, na_w_in, na_rpb, na_w_out, mla_w_in, mla_q_norm,
           mla_w_q_b, mla_kv_norm, mla_w_kv_b, mla_w_out):
    batch, seq, d = x.shape
    xt = x.reshape(batch * seq, d)
    cos_t, sin_t = _rope_tables(seq)
    h = rms_norm(xt, norm_pre[0].reshape(1, d), bm=512)
    for i in range(DEPTH):
        j = i // 2
        g_post = norm_post[i].reshape(1, d)
        g_next = norm_pre[i + 1].reshape(1, d) if i + 1 < DEPTH else None
        if i % 2 == 0:
            y = na_projection(h, na_w_in, j, sections=(3, 1), transposed=False,
                              scale_first=False, bm=1024, bn=1024)
            qvt = na_projection(h, na_w_in, j, sections=(0, 2), transposed=True,
                                scale_first=True, bm=1024, bn=1024)
            table = na_bias_table(na_rpb[j])
            o = na_attention(y, qvt, table, batch=batch, seq=seq, groups_per_step=4)
            w_out = na_w_out[j]
        else:
            y = matmul(h, _prep_mla_w_in(mla_w_in[j]), bm=1024, bn=MLA_IN_PAD // 2)
            wk, wvt = _prep_mla_w_kv(mla_w_kv_b[j])
            qt, k, vt = mla_projections(
                y, mla_q_norm[j].reshape(1, -1), mla_kv_norm[j].reshape(1, -1), cos_t, sin_t,
                _prep_mla_w_q(mla_w_q_b[j]).T, wk, wvt, batch=batch, seq=seq, bm=256)
            o = flash_attention(qt, k, vt, bq=1024, bkv=1024, first=256).reshape(batch * seq, MLA_WIDTH)
            w_out = mla_w_out[j]
        xt, h = gate_out(o, y, w_out.astype(BF16), xt, g_post, g_next, bm=512, n_chains=2)
    return xt.reshape(batch, seq, d)
```

```python
import functools
import math

import jax
import jax.numpy as jnp
from jax import lax
from jax.experimental import pallas as pl
from jax.experimental.pallas import tpu as pltpu

F32 = jnp.float32
BF16 = jnp.bfloat16

D_MODEL = 2048
DEPTH = 4
GRID_W = 64
RMS_EPS = 1e-6
LOG2E = math.log2(math.e)
NEG = -1e30

NA_HEADS = 16
NA_HEAD_DIM = 128
NA_WIDTH = NA_HEADS * NA_HEAD_DIM
NA_Q_SCALE = NA_HEAD_DIM ** -0.5 * LOG2E
NA_KH = 8
NA_KW = 16
NA_BIAS_ROWS = 2 * NA_KH - 1
NA_BIAS_COLS = 2 * NA_KW - 1
NA_TABLE_SLABS = NA_BIAS_ROWS + 1
NA_ROW_GROUP = 8
NA_KEY_ROWS = 16
NA_Q_TILE = NA_ROW_GROUP * GRID_W
NA_K_TILE = NA_KEY_ROWS * GRID_W
NA_MAX_EXCESS = 32.0

MLA_HEADS = 16
Q_LORA = 512
KV_LORA = 512
QK_NOPE = 128
QK_ROPE = 64
QK_HEAD = QK_NOPE + QK_ROPE
V_HEAD = 128
MLA_WIDTH = MLA_HEADS * V_HEAD
ROPE_THETA = 10000.0
MLA_QK_PAD = 256
MLA_V_EXT = V_HEAD + 16
FLASH_MAX_EXCESS = 32.0
MLA_IN_PAD = MLA_WIDTH + Q_LORA + KV_LORA + 256

VMEM_LIMIT_V7X = 56 * 1024 * 1024


def _params(semantics):
    return pltpu.CompilerParams(dimension_semantics=semantics, vmem_limit_bytes=VMEM_LIMIT_V7X)


def _rms(x, g):
    return x * lax.rsqrt(jnp.mean(x * x, axis=-1, keepdims=True) + RMS_EPS) * g


def _rms_norm_kernel(x_ref, g_ref, h_ref):
    h_ref[...] = _rms(x_ref[...], g_ref[...]).astype(h_ref.dtype)


def rms_norm(x, g, *, bm):
    t, d = x.shape
    return pl.pallas_call(
        _rms_norm_kernel,
        grid=(t // bm,),
        in_specs=[pl.BlockSpec((bm, d), lambda i: (i, 0)), pl.BlockSpec((1, d), lambda i: (0, 0))],
        out_specs=pl.BlockSpec((bm, d), lambda i: (i, 0)),
        out_shape=jax.ShapeDtypeStruct((t, d), BF16),
        compiler_params=_params(("arbitrary",)),
        name="rms_norm",
    )(x, g)


def _matmul_kernel(h_ref, w_ref, o_ref):
    o_ref[...] = jnp.dot(h_ref[...], w_ref[...], preferred_element_type=F32).astype(o_ref.dtype)


def matmul(h, w, *, bm, bn):
    t, d = h.shape
    n = w.shape[1]
    return pl.pallas_call(
        _matmul_kernel,
        grid=(n // bn, t // bm),
        in_specs=[pl.BlockSpec((bm, d), lambda j, i: (i, 0)), pl.BlockSpec((d, bn), lambda j, i: (0, j))],
        out_specs=pl.BlockSpec((bm, bn), lambda j, i: (i, j)),
        out_shape=jax.ShapeDtypeStruct((t, n), BF16),
        compiler_params=_params(("arbitrary", "arbitrary")),
        name="matmul",
    )(h, w)


def _na_proj_kernel(h_ref, w_ref, o_ref, wb_ref, *, transposed, scaled_blocks):
    j = pl.program_id(0)

    @pl.when(pl.program_id(1) == 0)
    def _():
        w = w_ref[0]
        wb_ref[...] = (w.T if transposed else w).astype(BF16)

    if transposed:
        res = lax.dot_general(wb_ref[...], h_ref[...], (((1,), (1,)), ((), ())), preferred_element_type=F32)
    else:
        res = jnp.dot(h_ref[...], wb_ref[...], preferred_element_type=F32)
    if scaled_blocks:
        res = res * jnp.where(j < scaled_blocks, NA_Q_SCALE, 1.0)
    o_ref[...] = res.astype(o_ref.dtype)


def na_projection(h, w_all, layer, *, sections, transposed, scale_first, bm, bn):
    t, d = h.shape
    nb = NA_WIDTH // bn
    sec_a, sec_b = sections

    def w_block(j, i):
        return layer, 0, jnp.where(j < nb, sec_a * nb + j, sec_b * nb + j - nb)

    n = 2 * NA_WIDTH
    if transposed:
        out_spec = pl.BlockSpec((bn, bm), lambda j, i: (j, i))
        out_shape = jax.ShapeDtypeStruct((n, t), BF16)
        wb_shape = (bn, d)
    else:
        out_spec = pl.BlockSpec((bm, bn), lambda j, i: (i, j))
        out_shape = jax.ShapeDtypeStruct((t, n), BF16)
        wb_shape = (d, bn)
    return pl.pallas_call(
        functools.partial(_na_proj_kernel, transposed=transposed, scaled_blocks=nb if scale_first else 0),
        grid=(n // bn, t // bm),
        in_specs=[pl.BlockSpec((bm, d), lambda j, i: (i, 0)), pl.BlockSpec((1, d, bn), w_block)],
        out_specs=out_spec,
        out_shape=out_shape,
        scratch_shapes=[pltpu.VMEM(wb_shape, BF16)],
        compiler_params=_params(("arbitrary", "arbitrary")),
        name="na_projection_t" if transposed else "na_projection",
    )(h, w_all)


def _gate_out_kernel(o_ref, z_ref, w_ref, x_ref, g_ref, *rest, n_chains, emit_next):
    if emit_next:
        g_next_ref, out_ref, h_next_ref = rest
    else:
        (out_ref,) = rest
    rows = o_ref.shape[0] // n_chains
    for c in range(n_chains):
        sl = slice(c * rows, (c + 1) * rows)
        z = z_ref[sl, :].astype(F32)
        gate = z * (1.0 / (1.0 + jnp.exp(-z)))
        a = (o_ref[sl, :].astype(F32) * gate).astype(BF16)
        y = jnp.dot(a, w_ref[...], preferred_element_type=F32)
        x_new = x_ref[sl, :] + _rms(y, g_ref[...])
        out_ref[sl, :] = x_new
        if emit_next:
            h_next_ref[sl, :] = _rms(x_new, g_next_ref[...]).astype(h_next_ref.dtype)


def gate_out(o, yz, w_out, x, g, g_next, *, bm, n_chains):
    t, d = x.shape
    emit_next = g_next is not None
    row_spec = pl.BlockSpec((bm, d), lambda i: (i, 0))
    gain_spec = pl.BlockSpec((1, d), lambda i: (0, 0))
    in_specs = [row_spec, row_spec, pl.BlockSpec((d, d), lambda i: (0, 0)), row_spec, gain_spec]
    args = [o, yz, w_out, x, g]
    out_specs = [row_spec]
    out_shape = [jax.ShapeDtypeStruct((t, d), F32)]
    if emit_next:
        in_specs.append(gain_spec)
        args.append(g_next)
        out_specs.append(row_spec)
        out_shape.append(jax.ShapeDtypeStruct((t, d), BF16))
    res = pl.pallas_call(
        functools.partial(_gate_out_kernel, n_chains=n_chains, emit_next=emit_next),
        grid=(t // bm,),
        in_specs=in_specs,
        out_specs=out_specs,
        out_shape=out_shape,
        compiler_params=_params(("arbitrary",)),
        name="gate_out",
    )(*args)
    return (res[0], res[1]) if emit_next else (res[0], None)


def _na_bias_table_kernel(rpb_ref, out_ref):
    h = pl.program_id(0)
    kc = lax.broadcasted_iota(jnp.int32, (GRID_W, 2 * GRID_W), 0)
    lane = lax.broadcasted_iota(jnp.int32, (GRID_W, 2 * GRID_W), 1)
    c = lane & (GRID_W - 1)
    diff = kc - c + (NA_KW - 1)
    col_start = jnp.clip(c - NA_KW // 2, 0, GRID_W - NA_KW)
    valid = (kc >= col_start) & (kc < col_start + NA_KW)
    neg = jnp.full((GRID_W, 2 * GRID_W), NEG, F32)

    def toeplitz(dr):
        base = (h * NA_BIAS_ROWS + dr) * NA_BIAS_COLS

        acc = jnp.zeros((GRID_W, 2 * GRID_W), F32)
        for d in range(NA_BIAS_COLS):
            acc = jnp.where(diff == d, rpb_ref[base + d], acc)
        return jnp.where(valid, acc * LOG2E, neg)

    slabs = [toeplitz(dr) for dr in range(NA_BIAS_ROWS)]

    def slab(dr):
        return slabs[dr] if 0 <= dr < NA_BIAS_ROWS else neg

    for t in range(NA_TABLE_SLABS):
        out_ref[0, t] = jnp.where(lane >= GRID_W, slab(t - 1), slab(t))


def na_bias_table(rpb):
    heads = rpb.shape[0]
    return pl.pallas_call(
        _na_bias_table_kernel,
        grid=(heads,),
        in_specs=[pl.BlockSpec(memory_space=pltpu.SMEM)],
        out_specs=pl.BlockSpec((1, NA_TABLE_SLABS, GRID_W, 2 * GRID_W), lambda h: (h, 0, 0, 0)),
        out_shape=jax.ShapeDtypeStruct((heads, NA_TABLE_SLABS, GRID_W, 2 * GRID_W), F32),
        compiler_params=_params(("arbitrary",)),
        name="na_bias_table",
    )(rpb.reshape(-1))


def _na_group_geometry(kind, rows):
    if kind == 0:
        return 0, 0
    if kind == 1:
        return NA_ROW_GROUP, NA_ROW_GROUP - NA_KH // 2
    return rows - NA_ROW_GROUP, rows - NA_KEY_ROWS


def _na_in_window(r, kr, rows):
    row_start = min(max(r - NA_KH // 2, 0), rows - NA_KH)
    return row_start <= kr < row_start + NA_KH


def _na_segments(kind, rows):
    r0, kr0 = _na_group_geometry(kind, rows)
    half_rows = NA_ROW_GROUP // 2
    half_lanes = NA_Q_TILE // 2

    def sees(i, quarter):
        return any(_na_in_window(r0 + i, kr0 + j, rows) for j in range(4 * quarter, 4 * quarter + 4))

    safe = next(a for a in range(4) if all(sees(i, a) for i in range(NA_ROW_GROUP)))
    others = []
    for a in range(4):
        need = [any(sees(i, a) for i in range(b * half_rows, (b + 1) * half_rows)) for b in range(2)]
        if a != safe and any(need):
            others.append((a, 0 if need[0] else half_lanes, NA_Q_TILE if need[1] else half_lanes))
    return safe, others


def _build_na_bias_tile(table_ref, bias_ref, kind, rows):
    r0, kr0 = _na_group_geometry(kind, rows)
    lane = lax.broadcasted_iota(jnp.int32, (GRID_W, 2 * GRID_W), 1)
    neg = jnp.full((GRID_W, 2 * GRID_W), NEG, F32)

    for j in range(NA_KEY_ROWS):
        kr = kr0 + j
        for ip in range(NA_ROW_GROUP // 2):
            r_l = r0 + 2 * ip
            ok_l = _na_in_window(r_l, kr, rows)
            ok_r = _na_in_window(r_l + 1, kr, rows)
            t = kr - r_l + (NA_KH - 1)
            if ok_l or ok_r:
                tile = table_ref[0, t]
                if not ok_l:
                    tile = jnp.where(lane >= GRID_W, tile, neg)
                elif not ok_r:
                    tile = jnp.where(lane >= GRID_W, neg, tile)
            else:
                tile = neg
            bias_ref[kind, j * GRID_W:(j + 1) * GRID_W, ip * 2 * GRID_W:(ip + 1) * 2 * GRID_W] = tile


def _na_attn_kernel(qt_ref, k_ref, vt_ref, table_ref, o_ref, bias_ref, *, rows, groups_per_step):
    b = pl.program_id(1)
    step = pl.program_id(2)
    n_groups = rows // NA_ROW_GROUP

    @pl.when((b == 0) & (step == 0))
    def _():
        for kind in range(3):
            _build_na_bias_tile(table_ref, bias_ref, kind, rows)

    n_steps = n_groups // groups_per_step
    quarter = NA_K_TILE // 4
    half = NA_Q_TILE // 2

    def group(u, kind):
        g = step * groups_per_step + u
        kr0 = jnp.clip(g * NA_ROW_GROUP - NA_KH // 2, 0, rows - NA_KEY_ROWS)
        qt = qt_ref[:, u * NA_Q_TILE:(u + 1) * NA_Q_TILE]

        def scores(a, n_quarters=1, lo=0, hi=NA_Q_TILE):
            start = pl.multiple_of(kr0 * GRID_W + a * quarter, quarter)
            s = jnp.dot(k_ref[pl.ds(start, n_quarters * quarter), :], qt[:, lo:hi], preferred_element_type=F32)
            return s + bias_ref[kind, a * quarter:(a + n_quarters) * quarter, lo:hi]

        def weighted(a, pt, n_quarters=1):
            start = pl.multiple_of(kr0 * GRID_W + a * quarter, quarter)
            width = n_quarters * quarter
            vt_ext = jnp.concatenate([vt_ref[:, pl.ds(start, width)], jnp.ones((16, width), BF16)], axis=0)
            return jnp.dot(vt_ext, pt.astype(BF16), preferred_element_type=F32)

        def store(acc):
            o_t = acc[:NA_HEAD_DIM] * (1.0 / acc[NA_HEAD_DIM:NA_HEAD_DIM + 1])
            o_ref[u * NA_Q_TILE:(u + 1) * NA_Q_TILE, :] = o_t.T.astype(o_ref.dtype)

        return scores, weighted, store

    def run(kinds):
        excess = []
        heads = []
        tails = []
        for u, kind in enumerate(kinds):
            scores, weighted, store = group(u, kind)
            safe, others = _na_segments(kind, rows)
            st = scores(safe)
            heads.append((st, jnp.max(st, axis=0, keepdims=True), [scores(a, lo=lo, hi=hi) for a, lo, hi in others]))
        for u, kind in enumerate(kinds):
            scores, weighted, store = group(u, kind)
            safe, others = _na_segments(kind, rows)
            st, m0, other_scores = heads[u]
            acc = weighted(safe, jnp.exp2(st - m0))
            acc_halves = [acc[:, :half], acc[:, half:]]
            seen_halves = [m0[:, :half], m0[:, half:]]
            for (a, lo, hi), st in zip(others, other_scores):
                seen = jnp.max(st, axis=0, keepdims=True)
                part = weighted(a, jnp.exp2(st - m0[:, lo:hi]))
                for b in range(2):
                    if lo <= b * half and (b + 1) * half <= hi:
                        cols = slice(b * half - lo, (b + 1) * half - lo)
                        acc_halves[b] = acc_halves[b] + part[:, cols]
                        seen_halves[b] = jnp.maximum(seen_halves[b], seen[:, cols])
            tails.append((store, jnp.concatenate(acc_halves, axis=1)))
            excess.append(jnp.concatenate(seen_halves, axis=1) - m0)
        for store, acc in tails:
            store(acc)

        @pl.when(jnp.max(jnp.concatenate(excess, axis=0)) > NA_MAX_EXCESS)
        def _():
            for u, kind in enumerate(kinds):
                scores, weighted, store = group(u, kind)
                s_all = scores(0, n_quarters=4)
                m = jnp.max(s_all, axis=0, keepdims=True)
                store(weighted(0, jnp.exp2(s_all - m), n_quarters=4))

    def kinds_of(has_first, has_last):
        last = groups_per_step - 1
        return tuple(0 if has_first and u == 0 else 2 if has_last and u == last else 1
                     for u in range(groups_per_step))

    if n_steps == 1:
        run(kinds_of(True, True))
    else:
        pl.when(step == 0)(lambda: run(kinds_of(True, False)))
        pl.when(step == n_steps - 1)(lambda: run(kinds_of(False, True)))
        if n_steps > 2:
            pl.when((step > 0) & (step < n_steps - 1))(lambda: run(kinds_of(False, False)))


def na_attention(zk, qvt, table, *, batch, seq, groups_per_step):
    rows = seq // GRID_W
    assert rows % (NA_ROW_GROUP * groups_per_step) == 0 and rows >= NA_KEY_ROWS
    n_steps = rows // NA_ROW_GROUP // groups_per_step
    bq = NA_Q_TILE * groups_per_step
    hb = NA_WIDTH // NA_HEAD_DIM
    return pl.pallas_call(
        functools.partial(_na_attn_kernel, rows=rows, groups_per_step=groups_per_step),
        grid=(NA_HEADS, batch, n_steps),
        in_specs=[
            pl.BlockSpec((NA_HEAD_DIM, bq), lambda h, b, s: (h, b * n_steps + s)),
            pl.BlockSpec((seq, NA_HEAD_DIM), lambda h, b, s: (b, hb + h)),
            pl.BlockSpec((NA_HEAD_DIM, seq), lambda h, b, s: (hb + h, b)),
            pl.BlockSpec((1, NA_TABLE_SLABS, GRID_W, 2 * GRID_W), lambda h, b, s: (h, 0, 0, 0)),
        ],
        out_specs=pl.BlockSpec((bq, NA_HEAD_DIM), lambda h, b, s: (b * n_steps + s, h)),
        out_shape=jax.ShapeDtypeStruct((batch * seq, NA_WIDTH), BF16),
        scratch_shapes=[pltpu.VMEM((3, NA_K_TILE, NA_Q_TILE), F32)],
        compiler_params=_params(("arbitrary", "arbitrary", "arbitrary")),
        name="na_attention",
    )(qvt, zk, qvt, table)


def _mla_proj_kernel(cq_ref, ckv_ref, kr_ref, gq_ref, gkv_ref, cos_ref, sin_ref, cos_t_ref, sin_t_ref,
                     wqt_ref, wk_ref, wvt_ref, qt_ref, k_ref, vt_ref):
    hq = _rms(cq_ref[...].astype(F32), gq_ref[...]).astype(BF16)
    hkv = _rms(ckv_ref[...].astype(F32), gkv_ref[...]).astype(BF16)
    kr = kr_ref[...].astype(F32)
    k_roped = (kr * cos_ref[...] + pltpu.roll(kr, QK_ROPE, 1) * sin_ref[...]).astype(BF16)
    cos_t = cos_t_ref[...]
    sin_t = sin_t_ref[...]
    nt = (((1,), (1,)), ((), ()))
    qt_all = lax.dot_general(wqt_ref[...], hq, nt, preferred_element_type=F32)
    kn_all = jnp.dot(hkv, wk_ref[...], preferred_element_type=F32)
    vt_all = lax.dot_general(wvt_ref[...], hkv, nt, preferred_element_type=F32)
    ones = jnp.ones((MLA_V_EXT - V_HEAD, hq.shape[0]), BF16)
    for h in range(MLA_HEADS):
        r0 = h * MLA_QK_PAD
        qt_ref[0, h, :QK_NOPE, :] = qt_all[r0:r0 + QK_NOPE].astype(BF16)
        y2 = qt_all[r0 + QK_NOPE:r0 + MLA_QK_PAD]
        partner = jnp.concatenate([y2[QK_ROPE:], y2[:QK_ROPE]], axis=0)
        qt_ref[0, h, QK_NOPE:, :] = (y2 * cos_t + partner * sin_t).astype(BF16)
        k_ref[0, h, :, :QK_NOPE] = kn_all[:, h * QK_NOPE:(h + 1) * QK_NOPE].astype(BF16)
        k_ref[0, h, :, QK_NOPE:] = k_roped
        vt_ref[0, h, :V_HEAD, :] = vt_all[h * V_HEAD:(h + 1) * V_HEAD, :].astype(BF16)
        vt_ref[0, h, V_HEAD:, :] = ones


def mla_projections(y, gq, gkv, cos, sin, wqt, wk, wvt, *, batch, seq, bm):
    nb = seq // bm
    cq_blk = MLA_WIDTH // Q_LORA
    kr_blk = (MLA_WIDTH + Q_LORA + KV_LORA) // 128
    const2 = lambda b, i: (0, 0)
    return pl.pallas_call(
        _mla_proj_kernel,
        grid=(batch, nb),
        in_specs=[
            pl.BlockSpec((bm, Q_LORA), lambda b, i: (b * nb + i, cq_blk)),
            pl.BlockSpec((bm, KV_LORA), lambda b, i: (b * nb + i, cq_blk + 1)),
            pl.BlockSpec((bm, 128), lambda b, i: (b * nb + i, kr_blk)),
            pl.BlockSpec((1, Q_LORA), const2),
            pl.BlockSpec((1, KV_LORA), const2),
            pl.BlockSpec((bm, 128), lambda b, i: (i, 0)),
            pl.BlockSpec((bm, 128), lambda b, i: (i, 0)),
            pl.BlockSpec((128, bm), lambda b, i: (0, i)),
            pl.BlockSpec((128, bm), lambda b, i: (0, i)),
            pl.BlockSpec(wqt.shape, const2),
            pl.BlockSpec(wk.shape, const2),
            pl.BlockSpec(wvt.shape, const2),
        ],
        out_specs=[
            pl.BlockSpec((1, MLA_HEADS, MLA_QK_PAD, bm), lambda b, i: (b, 0, 0, i)),
            pl.BlockSpec((1, MLA_HEADS, bm, MLA_QK_PAD), lambda b, i: (b, 0, i, 0)),
            pl.BlockSpec((1, MLA_HEADS, MLA_V_EXT, bm), lambda b, i: (b, 0, 0, i)),
        ],
        out_shape=[
            jax.ShapeDtypeStruct((batch, MLA_HEADS, MLA_QK_PAD, seq), BF16),
            jax.ShapeDtypeStruct((batch, MLA_HEADS, seq, MLA_QK_PAD), BF16),
            jax.ShapeDtypeStruct((batch, MLA_HEADS, MLA_V_EXT, seq), BF16),
        ],
        compiler_params=_params(("arbitrary", "arbitrary")),
        name="mla_projections",
    )(y, y, y, gq, gkv, cos, sin, cos.T, sin.T, wqt, wk, wvt)


def _flash_store(o_ref, acc):
    dv = o_ref.shape[2]
    o_t = acc[:dv] * (1.0 / acc[dv:dv + 1])
    o_ref[0] = o_t.T.astype(o_ref.dtype)


def _flash_kernel(qt_ref, k_ref, vt_ref, o_ref, m_ref, acc_ref, *, bkv, n_kv, first):
    qt = qt_ref[0, 0]
    seq = k_ref.shape[2]

    def scores(lo, hi):
        return jnp.dot(k_ref[0, 0, lo:hi, :], qt, preferred_element_type=F32)

    def weighted(lo, hi, pt):
        return jnp.dot(vt_ref[0, 0, :, lo:hi], pt.astype(BF16), preferred_element_type=F32)

    st = scores(0, first)
    m0 = jnp.max(st, axis=0, keepdims=True)
    acc = weighted(0, first, jnp.exp2(st - m0))
    seen = m0
    for lo in range(first, seq, bkv):
        hi = min(lo + bkv, seq)
        st = scores(lo, hi)
        seen = jnp.maximum(seen, jnp.max(st, axis=0, keepdims=True))
        acc = acc + weighted(lo, hi, jnp.exp2(st - m0))
    _flash_store(o_ref, acc)

    @pl.when(jnp.max(seen - m0) > FLASH_MAX_EXCESS)
    def _():
        m_ref[...] = jnp.full(m_ref.shape, -jnp.inf, F32)
        acc_ref[...] = jnp.zeros(acc_ref.shape, F32)

        def body(c, carry):
            start = pl.multiple_of(c * bkv, bkv)
            s = jnp.dot(k_ref[0, 0, pl.ds(start, bkv), :], qt, preferred_element_type=F32)
            m_old = m_ref[...]
            m_new = jnp.maximum(m_old, jnp.max(s, axis=0, keepdims=True))
            pv = jnp.dot(vt_ref[0, 0, :, pl.ds(start, bkv)], jnp.exp2(s - m_new).astype(BF16),
                         preferred_element_type=F32)
            acc_ref[...] = jnp.exp2(m_old - m_new) * acc_ref[...] + pv
            m_ref[...] = m_new
            return carry

        lax.fori_loop(0, n_kv, body, 0)
        _flash_store(o_ref, acc_ref[...])


def flash_attention(qt, k, vt, *, bq, bkv, first):
    batch, heads, dq, seq = qt.shape
    dv_ext = vt.shape[2]
    return pl.pallas_call(
        functools.partial(_flash_kernel, bkv=bkv, n_kv=seq // bkv, first=first),
        grid=(batch, heads, seq // bq),
        in_specs=[
            pl.BlockSpec((1, 1, dq, bq), lambda b, h, i: (b, h, 0, i)),
            pl.BlockSpec((1, 1, seq, dq), lambda b, h, i: (b, h, 0, 0)),
            pl.BlockSpec((1, 1, dv_ext, seq), lambda b, h, i: (b, h, 0, 0)),
        ],
        out_specs=pl.BlockSpec((1, bq, V_HEAD), lambda b, h, i: (b, i, h)),
        out_shape=jax.ShapeDtypeStruct((batch, seq, heads * V_HEAD), BF16),
        scratch_shapes=[pltpu.VMEM((1, bq), F32), pltpu.VMEM((dv_ext, bq), F32)],
        compiler_params=_params(("arbitrary", "arbitrary", "arbitrary")),
        name="flash_attention",
    )(qt, k, vt)


def _prep_mla_w_in(w):
    cq = w[:, :Q_LORA]
    ckv = w[:, Q_LORA:Q_LORA + KV_LORA]
    kr = w[:, Q_LORA + KV_LORA:Q_LORA + KV_LORA + QK_ROPE]
    z = w[:, Q_LORA + KV_LORA + QK_ROPE:]
    kr_e, kr_o = kr[:, 0::2], kr[:, 1::2]
    pad = jnp.zeros((w.shape[0], MLA_IN_PAD - MLA_WIDTH - Q_LORA - KV_LORA - 2 * QK_ROPE), w.dtype)
    return jnp.concatenate([z, cq, ckv, kr_e, kr_o, kr_o, kr_e, pad], axis=1).astype(BF16)


def _prep_mla_w_q(w):
    w = w.reshape(Q_LORA, MLA_HEADS, QK_HEAD) * (QK_HEAD ** -0.5 * LOG2E)
    nope, rope = w[..., :QK_NOPE], w[..., QK_NOPE:]
    r_e, r_o = rope[..., 0::2], rope[..., 1::2]
    ext = jnp.concatenate([nope, r_e, r_o, r_o, r_e], axis=-1)
    return ext.reshape(Q_LORA, MLA_HEADS * MLA_QK_PAD).astype(BF16)


def _prep_mla_w_kv(w):
    w = w.reshape(KV_LORA, MLA_HEADS, QK_NOPE + V_HEAD)
    wk = w[..., :QK_NOPE].reshape(KV_LORA, MLA_HEADS * QK_NOPE)
    wvt = w[..., QK_NOPE:].reshape(KV_LORA, MLA_HEADS * V_HEAD).T
    return wk.astype(BF16), wvt.astype(BF16)


def _rope_tables(length):
    inv_freq = 1.0 / (ROPE_THETA ** (jnp.arange(0, QK_ROPE, 2, dtype=F32) / QK_ROPE))
    ang = jnp.arange(length, dtype=F32)[:, None] * inv_freq[None, :]
    c, s = jnp.cos(ang), jnp.sin(ang)
    zeros = jnp.zeros((length, QK_ROPE), F32)
    return jnp.concatenate([c, c, zeros], axis=1), jnp.concatenate([-s, s, zeros], axis=1)


def kernel(x, norm_pre, norm_post, na_w_in, na_rpb, na_w_out, mla_w_in, mla_q_norm,
           mla_w_q_b, mla_kv_norm, mla_w_kv_b, mla_w_out):
    batch, seq, d = x.shape
    xt = x.reshape(batch * seq, d)
    cos_t, sin_t = _rope_tables(seq)
    h = rms_norm(xt, norm_pre[0].reshape(1, d), bm=512)
    for i in range(DEPTH):
        j = i // 2
        g_post = norm_post[i].reshape(1, d)
        g_next = norm_pre[i + 1].reshape(1, d) if i + 1 < DEPTH else None
        if i % 2 == 0:
            y = na_projection(h, na_w_in, j, sections=(3, 1), transposed=False,
                              scale_first=False, bm=1024, bn=1024)
            qvt = na_projection(h, na_w_in, j, sections=(0, 2), transposed=True,
                                scale_first=True, bm=1024, bn=1024)
            table = na_bias_table(na_rpb[j])
            o = na_attention(y, qvt, table, batch=batch, seq=seq, groups_per_step=8)
            w_out = na_w_out[j]
        else:
            y = matmul(h, _prep_mla_w_in(mla_w_in[j]), bm=1024, bn=MLA_IN_PAD // 2)
            wk, wvt = _prep_mla_w_kv(mla_w_kv_b[j])
            qt, k, vt = mla_projections(
                y, mla_q_norm[j].reshape(1, -1), mla_kv_norm[j].reshape(1, -1), cos_t, sin_t,
                _prep_mla_w_q(mla_w_q_b[j]).T, wk, wvt, batch=batch, seq=seq, bm=256)
            o = flash_attention(qt, k, vt, bq=1024, bkv=1024, first=256).reshape(batch * seq, MLA_WIDTH)
            w_out = mla_w_out[j]
        xt, h = gate_out(o, y, w_out.astype(BF16), xt, g_post, g_next, bm=512, n_chains=2)
    return xt.reshape(batch, seq, d)
```

```python
import functools
import math

import jax
import jax.numpy as jnp
from jax import lax
from jax.experimental import pallas as pl
from jax.experimental.pallas import tpu as pltpu

F32 = jnp.float32
BF16 = jnp.bfloat16

D_MODEL = 2048
DEPTH = 4
GRID_W = 64
RMS_EPS = 1e-6
LOG2E = math.log2(math.e)
NEG = -1e30

NA_HEADS = 16
NA_HEAD_DIM = 128
NA_WIDTH = NA_HEADS * NA_HEAD_DIM
NA_Q_SCALE = NA_HEAD_DIM ** -0.5 * LOG2E
NA_KH = 8
NA_KW = 16
NA_BIAS_ROWS = 2 * NA_KH - 1
NA_BIAS_COLS = 2 * NA_KW - 1
NA_TABLE_SLABS = NA_BIAS_ROWS + 1
NA_ROW_GROUP = 8
NA_KEY_ROWS = 16
NA_Q_TILE = NA_ROW_GROUP * GRID_W
NA_K_TILE = NA_KEY_ROWS * GRID_W
NA_MAX_EXCESS = 32.0

MLA_HEADS = 16
Q_LORA = 512
KV_LORA = 512
QK_NOPE = 128
QK_ROPE = 64
QK_HEAD = QK_NOPE + QK_ROPE
V_HEAD = 128
MLA_WIDTH = MLA_HEADS * V_HEAD
ROPE_THETA = 10000.0
MLA_QK_PAD = 256
MLA_V_EXT = V_HEAD + 16
FLASH_MAX_EXCESS = 32.0
MLA_IN_PAD = MLA_WIDTH + Q_LORA + KV_LORA + 256

VMEM_LIMIT_V7X = 56 * 1024 * 1024


def _params(semantics):
    return pltpu.CompilerParams(dimension_semantics=semantics, vmem_limit_bytes=VMEM_LIMIT_V7X)


def _rms(x, g):
    return x * lax.rsqrt(jnp.mean(x * x, axis=-1, keepdims=True) + RMS_EPS) * g


def _rms_norm_kernel(x_ref, g_ref, h_ref):
    h_ref[...] = _rms(x_ref[...], g_ref[...]).astype(h_ref.dtype)


def rms_norm(x, g, *, bm):
    t, d = x.shape
    return pl.pallas_call(
        _rms_norm_kernel,
        grid=(t // bm,),
        in_specs=[pl.BlockSpec((bm, d), lambda i: (i, 0)), pl.BlockSpec((1, d), lambda i: (0, 0))],
        out_specs=pl.BlockSpec((bm, d), lambda i: (i, 0)),
        out_shape=jax.ShapeDtypeStruct((t, d), BF16),
        compiler_params=_params(("arbitrary",)),
        name="rms_norm",
    )(x, g)


def _matmul_kernel(h_ref, w_ref, o_ref):
    o_ref[...] = jnp.dot(h_ref[...], w_ref[...], preferred_element_type=F32).astype(o_ref.dtype)


def matmul(h, w, *, bm, bn):
    t, d = h.shape
    n = w.shape[1]
    return pl.pallas_call(
        _matmul_kernel,
        grid=(n // bn, t // bm),
        in_specs=[pl.BlockSpec((bm, d), lambda j, i: (i, 0)), pl.BlockSpec((d, bn), lambda j, i: (0, j))],
        out_specs=pl.BlockSpec((bm, bn), lambda j, i: (i, j)),
        out_shape=jax.ShapeDtypeStruct((t, n), BF16),
        compiler_params=_params(("arbitrary", "arbitrary")),
        name="matmul",
    )(h, w)


def _na_proj_kernel(h_ref, w_ref, o_ref, wb_ref, *, transposed, scaled_blocks):
    j = pl.program_id(0)

    @pl.when(pl.program_id(1) == 0)
    def _():
        w = w_ref[0]
        wb_ref[...] = (w.T if transposed else w).astype(BF16)

    if transposed:
        res = lax.dot_general(wb_ref[...], h_ref[...], (((1,), (1,)), ((), ())), preferred_element_type=F32)
    else:
        res = jnp.dot(h_ref[...], wb_ref[...], preferred_element_type=F32)
    if scaled_blocks:
        res = res * jnp.where(j < scaled_blocks, NA_Q_SCALE, 1.0)
    o_ref[...] = res.astype(o_ref.dtype)


def na_projection(h, w_all, layer, *, sections, transposed, scale_first, bm, bn):
    t, d = h.shape
    nb = NA_WIDTH // bn
    sec_a, sec_b = sections

    def w_block(j, i):
        return layer, 0, jnp.where(j < nb, sec_a * nb + j, sec_b * nb + j - nb)

    n = 2 * NA_WIDTH
    if transposed:
        out_spec = pl.BlockSpec((bn, bm), lambda j, i: (j, i))
        out_shape = jax.ShapeDtypeStruct((n, t), BF16)
        wb_shape = (bn, d)
    else:
        out_spec = pl.BlockSpec((bm, bn), lambda j, i: (i, j))
        out_shape = jax.ShapeDtypeStruct((t, n), BF16)
        wb_shape = (d, bn)
    return pl.pallas_call(
        functools.partial(_na_proj_kernel, transposed=transposed, scaled_blocks=nb if scale_first else 0),
        grid=(n // bn, t // bm),
        in_specs=[pl.BlockSpec((bm, d), lambda j, i: (i, 0)), pl.BlockSpec((1, d, bn), w_block)],
        out_specs=out_spec,
        out_shape=out_shape,
        scratch_shapes=[pltpu.VMEM(wb_shape, BF16)],
        compiler_params=_params(("arbitrary", "arbitrary")),
        name="na_projection_t" if transposed else "na_projection",
    )(h, w_all)


def _gate_out_kernel(o_ref, z_ref, w_ref, x_ref, g_ref, *rest, n_chains, emit_next):
    if emit_next:
        g_next_ref, out_ref, h_next_ref = rest
    else:
        (out_ref,) = rest
    rows = o_ref.shape[0] // n_chains
    slices = [slice(c * rows, (c + 1) * rows) for c in range(n_chains)]
    gated = []
    for sl in slices:
        z = z_ref[sl, :].astype(F32)
        gate = z * (1.0 / (1.0 + jnp.exp(-z)))
        gated.append((o_ref[sl, :].astype(F32) * gate).astype(BF16))
    ys = [jnp.dot(a, w_ref[...], preferred_element_type=F32) for a in gated]
    for sl, y in zip(slices, ys):
        x_new = x_ref[sl, :] + _rms(y, g_ref[...])
        out_ref[sl, :] = x_new
        if emit_next:
            h_next_ref[sl, :] = _rms(x_new, g_next_ref[...]).astype(h_next_ref.dtype)


def gate_out(o, yz, w_out, x, g, g_next, *, bm, n_chains):
    t, d = x.shape
    emit_next = g_next is not None
    row_spec = pl.BlockSpec((bm, d), lambda i: (i, 0))
    gain_spec = pl.BlockSpec((1, d), lambda i: (0, 0))
    in_specs = [row_spec, row_spec, pl.BlockSpec((d, d), lambda i: (0, 0)), row_spec, gain_spec]
    args = [o, yz, w_out, x, g]
    out_specs = [row_spec]
    out_shape = [jax.ShapeDtypeStruct((t, d), F32)]
    if emit_next:
        in_specs.append(gain_spec)
        args.append(g_next)
        out_specs.append(row_spec)
        out_shape.append(jax.ShapeDtypeStruct((t, d), BF16))
    res = pl.pallas_call(
        functools.partial(_gate_out_kernel, n_chains=n_chains, emit_next=emit_next),
        grid=(t // bm,),
        in_specs=in_specs,
        out_specs=out_specs,
        out_shape=out_shape,
        compiler_params=_params(("arbitrary",)),
        name="gate_out",
    )(*args)
    return (res[0], res[1]) if emit_next else (res[0], None)


def _na_bias_table_kernel(rpb_ref, out_ref):
    h = pl.program_id(0)
    kc = lax.broadcasted_iota(jnp.int32, (GRID_W, 2 * GRID_W), 0)
    lane = lax.broadcasted_iota(jnp.int32, (GRID_W, 2 * GRID_W), 1)
    c = lane & (GRID_W - 1)
    diff = kc - c + (NA_KW - 1)
    col_start = jnp.clip(c - NA_KW // 2, 0, GRID_W - NA_KW)
    valid = (kc >= col_start) & (kc < col_start + NA_KW)
    neg = jnp.full((GRID_W, 2 * GRID_W), NEG, F32)

    def toeplitz(dr):
        base = (h * NA_BIAS_ROWS + dr) * NA_BIAS_COLS

        acc = jnp.zeros((GRID_W, 2 * GRID_W), F32)
        for d in range(NA_BIAS_COLS):
            acc = jnp.where(diff == d, rpb_ref[base + d], acc)
        return jnp.where(valid, acc * LOG2E, neg)

    slabs = [toeplitz(dr) for dr in range(NA_BIAS_ROWS)]

    def slab(dr):
        return slabs[dr] if 0 <= dr < NA_BIAS_ROWS else neg

    for t in range(NA_TABLE_SLABS):
        out_ref[0, t] = jnp.where(lane >= GRID_W, slab(t - 1), slab(t))


def na_bias_table(rpb):
    heads = rpb.shape[0]
    return pl.pallas_call(
        _na_bias_table_kernel,
        grid=(heads,),
        in_specs=[pl.BlockSpec(memory_space=pltpu.SMEM)],
        out_specs=pl.BlockSpec((1, NA_TABLE_SLABS, GRID_W, 2 * GRID_W), lambda h: (h, 0, 0, 0)),
        out_shape=jax.ShapeDtypeStruct((heads, NA_TABLE_SLABS, GRID_W, 2 * GRID_W), F32),
        compiler_params=_params(("arbitrary",)),
        name="na_bias_table",
    )(rpb.reshape(-1))


def _na_group_geometry(kind, rows):
    if kind == 0:
        return 0, 0
    if kind == 1:
        return NA_ROW_GROUP, NA_ROW_GROUP - NA_KH // 2
    return rows - NA_ROW_GROUP, rows - NA_KEY_ROWS


def _na_in_window(r, kr, rows):
    row_start = min(max(r - NA_KH // 2, 0), rows - NA_KH)
    return row_start <= kr < row_start + NA_KH


def _na_segments(kind, rows):
    r0, kr0 = _na_group_geometry(kind, rows)
    half_rows = NA_ROW_GROUP // 2
    half_lanes = NA_Q_TILE // 2

    def sees(i, quarter):
        return any(_na_in_window(r0 + i, kr0 + j, rows) for j in range(4 * quarter, 4 * quarter + 4))

    safe = next(a for a in range(4) if all(sees(i, a) for i in range(NA_ROW_GROUP)))
    others = []
    for a in range(4):
        need = [any(sees(i, a) for i in range(b * half_rows, (b + 1) * half_rows)) for b in range(2)]
        if a != safe and any(need):
            others.append((a, 0 if need[0] else half_lanes, NA_Q_TILE if need[1] else half_lanes))
    return safe, others


def _build_na_bias_tile(table_ref, bias_ref, kind, rows):
    r0, kr0 = _na_group_geometry(kind, rows)
    lane = lax.broadcasted_iota(jnp.int32, (GRID_W, 2 * GRID_W), 1)
    neg = jnp.full((GRID_W, 2 * GRID_W), NEG, F32)

    for j in range(NA_KEY_ROWS):
        kr = kr0 + j
        for ip in range(NA_ROW_GROUP // 2):
            r_l = r0 + 2 * ip
            ok_l = _na_in_window(r_l, kr, rows)
            ok_r = _na_in_window(r_l + 1, kr, rows)
            t = kr - r_l + (NA_KH - 1)
            if ok_l or ok_r:
                tile = table_ref[0, t]
                if not ok_l:
                    tile = jnp.where(lane >= GRID_W, tile, neg)
                elif not ok_r:
                    tile = jnp.where(lane >= GRID_W, neg, tile)
            else:
                tile = neg
            bias_ref[kind, j * GRID_W:(j + 1) * GRID_W, ip * 2 * GRID_W:(ip + 1) * 2 * GRID_W] = tile


def _na_attn_kernel(qt_ref, k_ref, vt_ref, table_ref, o_ref, bias_ref, *, rows, groups_per_step):
    b = pl.program_id(1)
    step = pl.program_id(2)
    n_groups = rows // NA_ROW_GROUP

    @pl.when((b == 0) & (step == 0))
    def _():
        for kind in range(3):
            _build_na_bias_tile(table_ref, bias_ref, kind, rows)

    n_steps = n_groups // groups_per_step
    quarter = NA_K_TILE // 4
    half = NA_Q_TILE // 2

    def group(u, kind):
        g = step * groups_per_step + u
        kr0 = jnp.clip(g * NA_ROW_GROUP - NA_KH // 2, 0, rows - NA_KEY_ROWS)
        qt = qt_ref[:, u * NA_Q_TILE:(u + 1) * NA_Q_TILE]

        def scores(a, n_quarters=1, lo=0, hi=NA_Q_TILE):
            start = pl.multiple_of(kr0 * GRID_W + a * quarter, quarter)
            s = jnp.dot(k_ref[pl.ds(start, n_quarters * quarter), :], qt[:, lo:hi], preferred_element_type=F32)
            return s + bias_ref[kind, a * quarter:(a + n_quarters) * quarter, lo:hi]

        def weighted(a, pt, n_quarters=1):
            start = pl.multiple_of(kr0 * GRID_W + a * quarter, quarter)
            width = n_quarters * quarter
            vt_ext = jnp.concatenate([vt_ref[:, pl.ds(start, width)], jnp.ones((16, width), BF16)], axis=0)
            return jnp.dot(vt_ext, pt.astype(BF16), preferred_element_type=F32)

        def store(acc):
            o_t = acc[:NA_HEAD_DIM] * (1.0 / acc[NA_HEAD_DIM:NA_HEAD_DIM + 1])
            o_ref[u * NA_Q_TILE:(u + 1) * NA_Q_TILE, :] = o_t.T.astype(o_ref.dtype)

        return scores, weighted, store

    def run(kinds):
        excess = []
        heads = []
        tails = []
        for u, kind in enumerate(kinds):
            scores, weighted, store = group(u, kind)
            safe, others = _na_segments(kind, rows)
            st = scores(safe)
            heads.append((st, jnp.max(st, axis=0, keepdims=True), [scores(a, lo=lo, hi=hi) for a, lo, hi in others]))
        for u, kind in enumerate(kinds):
            scores, weighted, store = group(u, kind)
            safe, others = _na_segments(kind, rows)
            st, m0, other_scores = heads[u]
            acc = weighted(safe, jnp.exp2(st - m0))
            acc_halves = [acc[:, :half], acc[:, half:]]
            seen_halves = [m0[:, :half], m0[:, half:]]
            for (a, lo, hi), st in zip(others, other_scores):
                seen = jnp.max(st, axis=0, keepdims=True)
                part = weighted(a, jnp.exp2(st - m0[:, lo:hi]))
                for b in range(2):
                    if lo <= b * half and (b + 1) * half <= hi:
                        cols = slice(b * half - lo, (b + 1) * half - lo)
                        acc_halves[b] = acc_halves[b] + part[:, cols]
                        seen_halves[b] = jnp.maximum(seen_halves[b], seen[:, cols])
            tails.append((store, jnp.concatenate(acc_halves, axis=1)))
            excess.append(jnp.concatenate(seen_halves, axis=1) - m0)
        for store, acc in tails:
            store(acc)

        @pl.when(jnp.max(jnp.concatenate(excess, axis=0)) > NA_MAX_EXCESS)
        def _():
            for u, kind in enumerate(kinds):
                scores, weighted, store = group(u, kind)
                s_all = scores(0, n_quarters=4)
                m = jnp.max(s_all, axis=0, keepdims=True)
                store(weighted(0, jnp.exp2(s_all - m), n_quarters=4))

    def kinds_of(has_first, has_last):
        last = groups_per_step - 1
        return tuple(0 if has_first and u == 0 else 2 if has_last and u == last else 1
                     for u in range(groups_per_step))

    if n_steps == 1:
        run(kinds_of(True, True))
    else:
        pl.when(step == 0)(lambda: run(kinds_of(True, False)))
        pl.when(step == n_steps - 1)(lambda: run(kinds_of(False, True)))
        if n_steps > 2:
            pl.when((step > 0) & (step < n_steps - 1))(lambda: run(kinds_of(False, False)))


def na_attention(zk, qvt, table, *, batch, seq, groups_per_step):
    rows = seq // GRID_W
    assert rows % (NA_ROW_GROUP * groups_per_step) == 0 and rows >= NA_KEY_ROWS
    n_steps = rows // NA_ROW_GROUP // groups_per_step
    bq = NA_Q_TILE * groups_per_step
    hb = NA_WIDTH // NA_HEAD_DIM
    return pl.pallas_call(
        functools.partial(_na_attn_kernel, rows=rows, groups_per_step=groups_per_step),
        grid=(NA_HEADS, batch, n_steps),
        in_specs=[
            pl.BlockSpec((NA_HEAD_DIM, bq), lambda h, b, s: (h, b * n_steps + s)),
            pl.BlockSpec((seq, NA_HEAD_DIM), lambda h, b, s: (b, hb + h)),
            pl.BlockSpec((NA_HEAD_DIM, seq), lambda h, b, s: (hb + h, b)),
            pl.BlockSpec((1, NA_TABLE_SLABS, GRID_W, 2 * GRID_W), lambda h, b, s: (h, 0, 0, 0)),
        ],
        out_specs=pl.BlockSpec((bq, NA_HEAD_DIM), lambda h, b, s: (b * n_steps + s, h)),
        out_shape=jax.ShapeDtypeStruct((batch * seq, NA_WIDTH), BF16),
        scratch_shapes=[pltpu.VMEM((3, NA_K_TILE, NA_Q_TILE), F32)],
        compiler_params=_params(("arbitrary", "arbitrary", "arbitrary")),
        name="na_attention",
    )(qvt, zk, qvt, table)


def _mla_proj_kernel(cq_ref, ckv_ref, kr_ref, gq_ref, gkv_ref, cos_ref, sin_ref, cos_t_ref, sin_t_ref,
                     wqt_ref, wk_ref, wvt_ref, qt_ref, k_ref, vt_ref):
    hq = _rms(cq_ref[...].astype(F32), gq_ref[...]).astype(BF16)
    hkv = _rms(ckv_ref[...].astype(F32), gkv_ref[...]).astype(BF16)
    kr = kr_ref[...].astype(F32)
    k_roped = (kr * cos_ref[...] + pltpu.roll(kr, QK_ROPE, 1) * sin_ref[...]).astype(BF16)
    cos_t = cos_t_ref[...]
    sin_t = sin_t_ref[...]
    nt = (((1,), (1,)), ((), ()))
    qt_all = lax.dot_general(wqt_ref[...], hq, nt, preferred_element_type=F32)
    kn_all = jnp.dot(hkv, wk_ref[...], preferred_element_type=F32)
    vt_all = lax.dot_general(wvt_ref[...], hkv, nt, preferred_element_type=F32)
    ones = jnp.ones((MLA_V_EXT - V_HEAD, hq.shape[0]), BF16)
    for h in range(MLA_HEADS):
        r0 = h * MLA_QK_PAD
        qt_ref[0, h, :QK_NOPE, :] = qt_all[r0:r0 + QK_NOPE].astype(BF16)
        y2 = qt_all[r0 + QK_NOPE:r0 + MLA_QK_PAD]
        partner = jnp.concatenate([y2[QK_ROPE:], y2[:QK_ROPE]], axis=0)
        qt_ref[0, h, QK_NOPE:, :] = (y2 * cos_t + partner * sin_t).astype(BF16)
        k_ref[0, h, :, :QK_NOPE] = kn_all[:, h * QK_NOPE:(h + 1) * QK_NOPE].astype(BF16)
        k_ref[0, h, :, QK_NOPE:] = k_roped
        vt_ref[0, h, :V_HEAD, :] = vt_all[h * V_HEAD:(h + 1) * V_HEAD, :].astype(BF16)
        vt_ref[0, h, V_HEAD:, :] = ones


def mla_projections(y, gq, gkv, cos, sin, wqt, wk, wvt, *, batch, seq, bm):
    nb = seq // bm
    cq_blk = MLA_WIDTH // Q_LORA
    kr_blk = (MLA_WIDTH + Q_LORA + KV_LORA) // 128
    const2 = lambda b, i: (0, 0)
    return pl.pallas_call(
        _mla_proj_kernel,
        grid=(batch, nb),
        in_specs=[
            pl.BlockSpec((bm, Q_LORA), lambda b, i: (b * nb + i, cq_blk)),
            pl.BlockSpec((bm, KV_LORA), lambda b, i: (b * nb + i, cq_blk + 1)),
            pl.BlockSpec((bm, 128), lambda b, i: (b * nb + i, kr_blk)),
            pl.BlockSpec((1, Q_LORA), const2),
            pl.BlockSpec((1, KV_LORA), const2),
            pl.BlockSpec((bm, 128), lambda b, i: (i, 0)),
            pl.BlockSpec((bm, 128), lambda b, i: (i, 0)),
            pl.BlockSpec((128, bm), lambda b, i: (0, i)),
            pl.BlockSpec((128, bm), lambda b, i: (0, i)),
            pl.BlockSpec(wqt.shape, const2),
            pl.BlockSpec(wk.shape, const2),
            pl.BlockSpec(wvt.shape, const2),
        ],
        out_specs=[
            pl.BlockSpec((1, MLA_HEADS, MLA_QK_PAD, bm), lambda b, i: (b, 0, 0, i)),
            pl.BlockSpec((1, MLA_HEADS, bm, MLA_QK_PAD), lambda b, i: (b, 0, i, 0)),
            pl.BlockSpec((1, MLA_HEADS, MLA_V_EXT, bm), lambda b, i: (b, 0, 0, i)),
        ],
        out_shape=[
            jax.ShapeDtypeStruct((batch, MLA_HEADS, MLA_QK_PAD, seq), BF16),
            jax.ShapeDtypeStruct((batch, MLA_HEADS, seq, MLA_QK_PAD), BF16),
            jax.ShapeDtypeStruct((batch, MLA_HEADS, MLA_V_EXT, seq), BF16),
        ],
        compiler_params=_params(("arbitrary", "arbitrary")),
        name="mla_projections",
    )(y, y, y, gq, gkv, cos, sin, cos.T, sin.T, wqt, wk, wvt)


def _flash_store(o_ref, acc):
    dv = o_ref.shape[2]
    o_t = acc[:dv] * (1.0 / acc[dv:dv + 1])
    o_ref[0] = o_t.T.astype(o_ref.dtype)


def _flash_kernel(qt_ref, k_ref, vt_ref, o_ref, m_ref, acc_ref, *, bkv, n_kv, first):
    qt = qt_ref[0, 0]
    seq = k_ref.shape[2]

    def scores(lo, hi):
        return jnp.dot(k_ref[0, 0, lo:hi, :], qt, preferred_element_type=F32)

    def weighted(lo, hi, pt):
        return jnp.dot(vt_ref[0, 0, :, lo:hi], pt.astype(BF16), preferred_element_type=F32)

    bounds = [(0, first)] + [(lo, min(lo + bkv, seq)) for lo in range(first, seq, bkv)]
    st = scores(*bounds[0])
    m0 = jnp.max(st, axis=0, keepdims=True)
    seen = m0
    acc = None
    for c, (lo, hi) in enumerate(bounds):
        st_next = scores(*bounds[c + 1]) if c + 1 < len(bounds) else None
        if c > 0:
            seen = jnp.maximum(seen, jnp.max(st, axis=0, keepdims=True))
        part = weighted(lo, hi, jnp.exp2(st - m0))
        acc = part if acc is None else acc + part
        st = st_next
    _flash_store(o_ref, acc)

    @pl.when(jnp.max(seen - m0) > FLASH_MAX_EXCESS)
    def _():
        m_ref[...] = jnp.full(m_ref.shape, -jnp.inf, F32)
        acc_ref[...] = jnp.zeros(acc_ref.shape, F32)

        def body(c, carry):
            start = pl.multiple_of(c * bkv, bkv)
            s = jnp.dot(k_ref[0, 0, pl.ds(start, bkv), :], qt, preferred_element_type=F32)
            m_old = m_ref[...]
            m_new = jnp.maximum(m_old, jnp.max(s, axis=0, keepdims=True))
            pv = jnp.dot(vt_ref[0, 0, :, pl.ds(start, bkv)], jnp.exp2(s - m_new).astype(BF16),
                         preferred_element_type=F32)
            acc_ref[...] = jnp.exp2(m_old - m_new) * acc_ref[...] + pv
            m_ref[...] = m_new
            return carry

        lax.fori_loop(0, n_kv, body, 0)
        _flash_store(o_ref, acc_ref[...])


def flash_attention(qt, k, vt, *, bq, bkv, first):
    batch, heads, dq, seq = qt.shape
    dv_ext = vt.shape[2]
    return pl.pallas_call(
        functools.partial(_flash_kernel, bkv=bkv, n_kv=seq // bkv, first=first),
        grid=(batch, heads, seq // bq),
        in_specs=[
            pl.BlockSpec((1, 1, dq, bq), lambda b, h, i: (b, h, 0, i)),
            pl.BlockSpec((1, 1, seq, dq), lambda b, h, i: (b, h, 0, 0)),
            pl.BlockSpec((1, 1, dv_ext, seq), lambda b, h, i: (b, h, 0, 0)),
        ],
        out_specs=pl.BlockSpec((1, bq, V_HEAD), lambda b, h, i: (b, i, h)),
        out_shape=jax.ShapeDtypeStruct((batch, seq, heads * V_HEAD), BF16),
        scratch_shapes=[pltpu.VMEM((1, bq), F32), pltpu.VMEM((dv_ext, bq), F32)],
        compiler_params=_params(("arbitrary", "arbitrary", "arbitrary")),
        name="flash_attention",
    )(qt, k, vt)


def _prep_mla_w_in(w):
    cq = w[:, :Q_LORA]
    ckv = w[:, Q_LORA:Q_LORA + KV_LORA]
    kr = w[:, Q_LORA + KV_LORA:Q_LORA + KV_LORA + QK_ROPE]
    z = w[:, Q_LORA + KV_LORA + QK_ROPE:]
    kr_e, kr_o = kr[:, 0::2], kr[:, 1::2]
    pad = jnp.zeros((w.shape[0], MLA_IN_PAD - MLA_WIDTH - Q_LORA - KV_LORA - 2 * QK_ROPE), w.dtype)
    return jnp.concatenate([z, cq, ckv, kr_e, kr_o, kr_o, kr_e, pad], axis=1).astype(BF16)


def _prep_mla_w_q(w):
    w = w.reshape(Q_LORA, MLA_HEADS, QK_HEAD) * (QK_HEAD ** -0.5 * LOG2E)
    nope, rope = w[..., :QK_NOPE], w[..., QK_NOPE:]
    r_e, r_o = rope[..., 0::2], rope[..., 1::2]
    ext = jnp.concatenate([nope, r_e, r_o, r_o, r_e], axis=-1)
    return ext.reshape(Q_LORA, MLA_HEADS * MLA_QK_PAD).astype(BF16)


def _prep_mla_w_kv(w):
    w = w.reshape(KV_LORA, MLA_HEADS, QK_NOPE + V_HEAD)
    wk = w[..., :QK_NOPE].reshape(KV_LORA, MLA_HEADS * QK_NOPE)
    wvt = w[..., QK_NOPE:].reshape(KV_LORA, MLA_HEADS * V_HEAD).T
    return wk.astype(BF16), wvt.astype(BF16)


def _rope_tables(length):
    inv_freq = 1.0 / (ROPE_THETA ** (jnp.arange(0, QK_ROPE, 2, dtype=F32) / QK_ROPE))
    ang = jnp.arange(length, dtype=F32)[:, None] * inv_freq[None, :]
    c, s = jnp.cos(ang), jnp.sin(ang)
    zeros = jnp.zeros((length, QK_ROPE), F32)
    return jnp.concatenate([c, c, zeros], axis=1), jnp.concatenate([-s, s, zeros], axis=1)


def kernel(x, norm_pre, norm_post, na_w_in, na_rpb, na_w_out, mla_w_in, mla_q_norm,
           mla_w_q_b, mla_kv_norm, mla_w_kv_b, mla_w_out):
    batch, seq, d = x.shape
    xt = x.reshape(batch * seq, d)
    cos_t, sin_t = _rope_tables(seq)
    h = rms_norm(xt, norm_pre[0].reshape(1, d), bm=512)
    for i in range(DEPTH):
        j = i // 2
        g_post = norm_post[i].reshape(1, d)
        g_next = norm_pre[i + 1].reshape(1, d) if i + 1 < DEPTH else None
        if i % 2 == 0:
            y = na_projection(h, na_w_in, j, sections=(3, 1), transposed=False,
                              scale_first=False, bm=1024, bn=1024)
            qvt = na_projection(h, na_w_in, j, sections=(0, 2), transposed=True,
                                scale_first=True, bm=1024, bn=1024)
            table = na_bias_table(na_rpb[j])
            o = na_attention(y, qvt, table, batch=batch, seq=seq, groups_per_step=8)
            w_out = na_w_out[j]
        else:
            y = matmul(h, _prep_mla_w_in(mla_w_in[j]), bm=1024, bn=MLA_IN_PAD // 2)
            wk, wvt = _prep_mla_w_kv(mla_w_kv_b[j])
            qt, k, vt = mla_projections(
                y, mla_q_norm[j].reshape(1, -1), mla_kv_norm[j].reshape(1, -1), cos_t, sin_t,
                _prep_mla_w_q(mla_w_q_b[j]).T, wk, wvt, batch=batch, seq=seq, bm=256)
            o = flash_attention(qt, k, vt, bq=1024, bkv=2048, first=256).reshape(batch * seq, MLA_WIDTH)
            w_out = mla_w_out[j]
        xt, h = gate_out(o, y, w_out.astype(BF16), xt, g_post, g_next, bm=512, n_chains=4)
    return xt.reshape(batch, seq, d)
```

```python
import functools
import math

import jax
import jax.numpy as jnp
from jax import lax
from jax.experimental import pallas as pl
from jax.experimental.pallas import tpu as pltpu

F32 = jnp.float32
BF16 = jnp.bfloat16
LANES = 128
BF16_SUBLANES = 16

D_MODEL = 2048
DEPTH = 4
GRID_W = 64
RMS_EPS = 1e-6
LOG2E = math.log2(math.e)
NEG = -1e30

NA_HEADS = 16
NA_HEAD_DIM = 128
NA_WIDTH = NA_HEADS * NA_HEAD_DIM
NA_Q_SCALE = NA_HEAD_DIM ** -0.5 * LOG2E
NA_KH = 8
NA_KW = 16
NA_BIAS_ROWS = 2 * NA_KH - 1
NA_BIAS_COLS = 2 * NA_KW - 1
NA_TABLE_SLABS = NA_BIAS_ROWS + 1
NA_ROW_GROUP = 8
NA_KEY_ROWS = 16
NA_Q_TILE = NA_ROW_GROUP * GRID_W
NA_K_TILE = NA_KEY_ROWS * GRID_W
NA_MAX_EXCESS = 32.0

MLA_HEADS = 16
Q_LORA = 512
KV_LORA = 512
QK_NOPE = 128
QK_ROPE = 64
QK_HEAD = QK_NOPE + QK_ROPE
V_HEAD = 128
MLA_WIDTH = MLA_HEADS * V_HEAD
ROPE_THETA = 10000.0
MLA_QK_PAD = 256
MLA_V_EXT = V_HEAD + BF16_SUBLANES
FLASH_MAX_EXCESS = 32.0
MLA_IN_PAD = MLA_WIDTH + Q_LORA + KV_LORA + 2 * LANES

VMEM_LIMIT_V7X = 56 * 1024 * 1024

TILES = dict(
    rms_rows=512,
    proj_rows=1024, proj_cols=1024,
    mla_in_rows=1024,
    mla_proj_rows=256,
    na_groups=8,
    flash_q=2048, flash_kv=1024, flash_first=256,
    gate_rows=512, gate_chains=4,
)


def _params(semantics):
    return pltpu.CompilerParams(dimension_semantics=semantics, vmem_limit_bytes=VMEM_LIMIT_V7X)


def _rms(x, g):
    return x * lax.rsqrt(jnp.mean(x * x, axis=-1, keepdims=True) + RMS_EPS) * g


def _rms_norm_kernel(x_ref, g_ref, h_ref):
    h_ref[...] = _rms(x_ref[...], g_ref[...]).astype(h_ref.dtype)


def rms_norm(x, g, *, bm):
    t, d = x.shape
    return pl.pallas_call(
        _rms_norm_kernel,
        grid=(t // bm,),
        in_specs=[pl.BlockSpec((bm, d), lambda i: (i, 0)), pl.BlockSpec((1, d), lambda i: (0, 0))],
        out_specs=pl.BlockSpec((bm, d), lambda i: (i, 0)),
        out_shape=jax.ShapeDtypeStruct((t, d), BF16),
        compiler_params=_params(("arbitrary",)),
        name="rms_norm",
    )(x, g)


def _matmul_kernel(h_ref, w_ref, o_ref):
    o_ref[...] = jnp.dot(h_ref[...], w_ref[...], preferred_element_type=F32).astype(o_ref.dtype)


def matmul(h, w, *, bm, bn):
    t, d = h.shape
    n = w.shape[1]
    return pl.pallas_call(
        _matmul_kernel,
        grid=(n // bn, t // bm),
        in_specs=[pl.BlockSpec((bm, d), lambda j, i: (i, 0)), pl.BlockSpec((d, bn), lambda j, i: (0, j))],
        out_specs=pl.BlockSpec((bm, bn), lambda j, i: (i, j)),
        out_shape=jax.ShapeDtypeStruct((t, n), BF16),
        compiler_params=_params(("arbitrary", "arbitrary")),
        name="matmul",
    )(h, w)


def _na_proj_kernel(h_ref, w_ref, o_ref, wb_ref, *, transposed, scaled_blocks):
    j = pl.program_id(0)

    @pl.when(pl.program_id(1) == 0)
    def _():
        w = w_ref[0]
        wb_ref[...] = (w.T if transposed else w).astype(BF16)

    if transposed:
        res = lax.dot_general(wb_ref[...], h_ref[...], (((1,), (1,)), ((), ())), preferred_element_type=F32)
    else:
        res = jnp.dot(h_ref[...], wb_ref[...], preferred_element_type=F32)
    if scaled_blocks:
        res = res * jnp.where(j < scaled_blocks, NA_Q_SCALE, 1.0)
    o_ref[...] = res.astype(o_ref.dtype)


def na_projection(h, w_all, layer, *, sections, transposed, scale_first, bm, bn):
    t, d = h.shape
    nb = NA_WIDTH // bn
    sec_a, sec_b = sections

    def w_block(j, i):
        return layer, 0, jnp.where(j < nb, sec_a * nb + j, sec_b * nb + j - nb)

    n = 2 * NA_WIDTH
    if transposed:
        out_spec = pl.BlockSpec((bn, bm), lambda j, i: (j, i))
        out_shape = jax.ShapeDtypeStruct((n, t), BF16)
        wb_shape = (bn, d)
    else:
        out_spec = pl.BlockSpec((bm, bn), lambda j, i: (i, j))
        out_shape = jax.ShapeDtypeStruct((t, n), BF16)
        wb_shape = (d, bn)
    return pl.pallas_call(
        functools.partial(_na_proj_kernel, transposed=transposed, scaled_blocks=nb if scale_first else 0),
        grid=(n // bn, t // bm),
        in_specs=[pl.BlockSpec((bm, d), lambda j, i: (i, 0)), pl.BlockSpec((1, d, bn), w_block)],
        out_specs=out_spec,
        out_shape=out_shape,
        scratch_shapes=[pltpu.VMEM(wb_shape, BF16)],
        compiler_params=_params(("arbitrary", "arbitrary")),
        name="na_projection_t" if transposed else "na_projection",
    )(h, w_all)


def _gate_out_kernel(o_ref, z_ref, w_ref, x_ref, g_ref, *rest, n_chains, emit_next):
    if emit_next:
        g_next_ref, out_ref, h_next_ref = rest
    else:
        (out_ref,) = rest
    rows = o_ref.shape[0] // n_chains
    slices = [slice(c * rows, (c + 1) * rows) for c in range(n_chains)]
    gated = []
    for sl in slices:
        z = z_ref[sl, :].astype(F32)
        gate = z * (1.0 / (1.0 + jnp.exp(-z)))
        gated.append((o_ref[sl, :].astype(F32) * gate).astype(BF16))
    ys = [jnp.dot(a, w_ref[...], preferred_element_type=F32) for a in gated]
    for sl, y in zip(slices, ys):
        x_new = x_ref[sl, :] + _rms(y, g_ref[...])
        out_ref[sl, :] = x_new
        if emit_next:
            h_next_ref[sl, :] = _rms(x_new, g_next_ref[...]).astype(h_next_ref.dtype)


def gate_out(o, yz, w_out, x, g, g_next, *, bm, n_chains):
    t, d = x.shape
    emit_next = g_next is not None
    row_spec = pl.BlockSpec((bm, d), lambda i: (i, 0))
    gain_spec = pl.BlockSpec((1, d), lambda i: (0, 0))
    in_specs = [row_spec, row_spec, pl.BlockSpec((d, d), lambda i: (0, 0)), row_spec, gain_spec]
    args = [o, yz, w_out, x, g]
    out_specs = [row_spec]
    out_shape = [jax.ShapeDtypeStruct((t, d), F32)]
    if emit_next:
        in_specs.append(gain_spec)
        args.append(g_next)
        out_specs.append(row_spec)
        out_shape.append(jax.ShapeDtypeStruct((t, d), BF16))
    res = pl.pallas_call(
        functools.partial(_gate_out_kernel, n_chains=n_chains, emit_next=emit_next),
        grid=(t // bm,),
        in_specs=in_specs,
        out_specs=out_specs,
        out_shape=out_shape,
        compiler_params=_params(("arbitrary",)),
        name="gate_out",
    )(*args)
    return (res[0], res[1]) if emit_next else (res[0], None)


def _na_bias_table_kernel(rpb_ref, out_ref):
    h = pl.program_id(0)
    kc = lax.broadcasted_iota(jnp.int32, (GRID_W, 2 * GRID_W), 0)
    lane = lax.broadcasted_iota(jnp.int32, (GRID_W, 2 * GRID_W), 1)
    c = lane & (GRID_W - 1)
    diff = kc - c + (NA_KW - 1)
    col_start = jnp.clip(c - NA_KW // 2, 0, GRID_W - NA_KW)
    valid = (kc >= col_start) & (kc < col_start + NA_KW)
    neg = jnp.full((GRID_W, 2 * GRID_W), NEG, F32)

    def toeplitz(dr):
        base = (h * NA_BIAS_ROWS + dr) * NA_BIAS_COLS

        acc = jnp.zeros((GRID_W, 2 * GRID_W), F32)
        for d in range(NA_BIAS_COLS):
            acc = jnp.where(diff == d, rpb_ref[base + d], acc)
        return jnp.where(valid, acc * LOG2E, neg)

    slabs = [toeplitz(dr) for dr in range(NA_BIAS_ROWS)]

    def slab(dr):
        return slabs[dr] if 0 <= dr < NA_BIAS_ROWS else neg

    for t in range(NA_TABLE_SLABS):
        out_ref[0, t] = jnp.where(lane >= GRID_W, slab(t - 1), slab(t))


def na_bias_table(rpb):
    heads = rpb.shape[0]
    return pl.pallas_call(
        _na_bias_table_kernel,
        grid=(heads,),
        in_specs=[pl.BlockSpec(memory_space=pltpu.SMEM)],
        out_specs=pl.BlockSpec((1, NA_TABLE_SLABS, GRID_W, 2 * GRID_W), lambda h: (h, 0, 0, 0)),
        out_shape=jax.ShapeDtypeStruct((heads, NA_TABLE_SLABS, GRID_W, 2 * GRID_W), F32),
        compiler_params=_params(("arbitrary",)),
        name="na_bias_table",
    )(rpb.reshape(-1))


def _na_group_geometry(kind, rows):
    if kind == 0:
        return 0, 0
    if kind == 1:
        return NA_ROW_GROUP, NA_ROW_GROUP - NA_KH // 2
    return rows - NA_ROW_GROUP, rows - NA_KEY_ROWS


def _na_in_window(r, kr, rows):
    row_start = min(max(r - NA_KH // 2, 0), rows - NA_KH)
    return row_start <= kr < row_start + NA_KH


def _na_segments(kind, rows):
    r0, kr0 = _na_group_geometry(kind, rows)
    half_rows = NA_ROW_GROUP // 2
    half_lanes = NA_Q_TILE // 2

    def sees(i, quarter):
        return any(_na_in_window(r0 + i, kr0 + j, rows) for j in range(4 * quarter, 4 * quarter + 4))

    safe = next(a for a in range(4) if all(sees(i, a) for i in range(NA_ROW_GROUP)))
    others = []
    for a in range(4):
        need = [any(sees(i, a) for i in range(b * half_rows, (b + 1) * half_rows)) for b in range(2)]
        if a != safe and any(need):
            others.append((a, 0 if need[0] else half_lanes, NA_Q_TILE if need[1] else half_lanes))
    return safe, others


def _build_na_bias_tile(table_ref, bias_ref, kind, rows):
    r0, kr0 = _na_group_geometry(kind, rows)
    lane = lax.broadcasted_iota(jnp.int32, (GRID_W, 2 * GRID_W), 1)
    neg = jnp.full((GRID_W, 2 * GRID_W), NEG, F32)

    for j in range(NA_KEY_ROWS):
        kr = kr0 + j
        for ip in range(NA_ROW_GROUP // 2):
            r_l = r0 + 2 * ip
            ok_l = _na_in_window(r_l, kr, rows)
            ok_r = _na_in_window(r_l + 1, kr, rows)
            t = kr - r_l + (NA_KH - 1)
            if ok_l or ok_r:
                tile = table_ref[0, t]
                if not ok_l:
                    tile = jnp.where(lane >= GRID_W, tile, neg)
                elif not ok_r:
                    tile = jnp.where(lane >= GRID_W, neg, tile)
            else:
                tile = neg
            bias_ref[kind, j * GRID_W:(j + 1) * GRID_W, ip * 2 * GRID_W:(ip + 1) * 2 * GRID_W] = tile


def _na_attn_kernel(qt_ref, k_ref, vt_ref, table_ref, o_ref, bias_ref, *, rows, groups_per_step):
    b = pl.program_id(1)
    step = pl.program_id(2)
    n_groups = rows // NA_ROW_GROUP

    @pl.when((b == 0) & (step == 0))
    def _():
        for kind in range(3):
            _build_na_bias_tile(table_ref, bias_ref, kind, rows)

    n_steps = n_groups // groups_per_step
    quarter = NA_K_TILE // 4
    half = NA_Q_TILE // 2

    def group(u, kind):
        g = step * groups_per_step + u
        kr0 = jnp.clip(g * NA_ROW_GROUP - NA_KH // 2, 0, rows - NA_KEY_ROWS)
        qt = qt_ref[:, u * NA_Q_TILE:(u + 1) * NA_Q_TILE]

        def scores(a, n_quarters=1, lo=0, hi=NA_Q_TILE):
            start = pl.multiple_of(kr0 * GRID_W + a * quarter, quarter)
            s = jnp.dot(k_ref[pl.ds(start, n_quarters * quarter), :], qt[:, lo:hi], preferred_element_type=F32)
            return s + bias_ref[kind, a * quarter:(a + n_quarters) * quarter, lo:hi]

        def weighted(a, pt, n_quarters=1):
            start = pl.multiple_of(kr0 * GRID_W + a * quarter, quarter)
            width = n_quarters * quarter
            vt_ext = jnp.concatenate([vt_ref[:, pl.ds(start, width)], jnp.ones((BF16_SUBLANES, width), BF16)], axis=0)
            return jnp.dot(vt_ext, pt.astype(BF16), preferred_element_type=F32)

        def store(acc):
            o_t = acc[:NA_HEAD_DIM] * (1.0 / acc[NA_HEAD_DIM:NA_HEAD_DIM + 1])
            o_ref[u * NA_Q_TILE:(u + 1) * NA_Q_TILE, :] = o_t.T.astype(o_ref.dtype)

        return scores, weighted, store

    def run(kinds):
        excess = []
        heads = []
        tails = []
        for u, kind in enumerate(kinds):
            scores, weighted, store = group(u, kind)
            safe, others = _na_segments(kind, rows)
            st = scores(safe)
            heads.append((st, jnp.max(st, axis=0, keepdims=True), [scores(a, lo=lo, hi=hi) for a, lo, hi in others]))
        for u, kind in enumerate(kinds):
            scores, weighted, store = group(u, kind)
            safe, others = _na_segments(kind, rows)
            st, m0, other_scores = heads[u]
            acc = weighted(safe, jnp.exp2(st - m0))
            acc_halves = [acc[:, :half], acc[:, half:]]
            seen_halves = [m0[:, :half], m0[:, half:]]
            for (a, lo, hi), st in zip(others, other_scores):
                seen = jnp.max(st, axis=0, keepdims=True)
                part = weighted(a, jnp.exp2(st - m0[:, lo:hi]))
                for b in range(2):
                    if lo <= b * half and (b + 1) * half <= hi:
                        cols = slice(b * half - lo, (b + 1) * half - lo)
                        acc_halves[b] = acc_halves[b] + part[:, cols]
                        seen_halves[b] = jnp.maximum(seen_halves[b], seen[:, cols])
            tails.append((store, jnp.concatenate(acc_halves, axis=1)))
            excess.append(jnp.concatenate(seen_halves, axis=1) - m0)
        for store, acc in tails:
            store(acc)

        @pl.when(jnp.max(jnp.concatenate(excess, axis=0)) > NA_MAX_EXCESS)
        def _():
            for u, kind in enumerate(kinds):
                scores, weighted, store = group(u, kind)
                s_all = scores(0, n_quarters=4)
                m = jnp.max(s_all, axis=0, keepdims=True)
                store(weighted(0, jnp.exp2(s_all - m), n_quarters=4))

    def kinds_of(has_first, has_last):
        last = groups_per_step - 1
        return tuple(0 if has_first and u == 0 else 2 if has_last and u == last else 1
                     for u in range(groups_per_step))

    if n_steps == 1:
        run(kinds_of(True, True))
    else:
        pl.when(step == 0)(lambda: run(kinds_of(True, False)))
        pl.when(step == n_steps - 1)(lambda: run(kinds_of(False, True)))
        if n_steps > 2:
            pl.when((step > 0) & (step < n_steps - 1))(lambda: run(kinds_of(False, False)))


def na_attention(zk, qvt, table, *, batch, seq, groups_per_step):
    rows = seq // GRID_W
    assert rows % (NA_ROW_GROUP * groups_per_step) == 0 and rows >= NA_KEY_ROWS
    n_steps = rows // NA_ROW_GROUP // groups_per_step
    bq = NA_Q_TILE * groups_per_step
    hb = NA_WIDTH // NA_HEAD_DIM
    return pl.pallas_call(
        functools.partial(_na_attn_kernel, rows=rows, groups_per_step=groups_per_step),
        grid=(NA_HEADS, batch, n_steps),
        in_specs=[
            pl.BlockSpec((NA_HEAD_DIM, bq), lambda h, b, s: (h, b * n_steps + s)),
            pl.BlockSpec((seq, NA_HEAD_DIM), lambda h, b, s: (b, hb + h)),
            pl.BlockSpec((NA_HEAD_DIM, seq), lambda h, b, s: (hb + h, b)),
            pl.BlockSpec((1, NA_TABLE_SLABS, GRID_W, 2 * GRID_W), lambda h, b, s: (h, 0, 0, 0)),
        ],
        out_specs=pl.BlockSpec((bq, NA_HEAD_DIM), lambda h, b, s: (b * n_steps + s, h)),
        out_shape=jax.ShapeDtypeStruct((batch * seq, NA_WIDTH), BF16),
        scratch_shapes=[pltpu.VMEM((3, NA_K_TILE, NA_Q_TILE), F32)],
        compiler_params=_params(("arbitrary", "arbitrary", "arbitrary")),
        name="na_attention",
    )(qvt, zk, qvt, table)


def _mla_proj_kernel(cq_ref, ckv_ref, kr_ref, gq_ref, gkv_ref, cos_ref, sin_ref, cos_t_ref, sin_t_ref,
                     wqt_ref, wk_ref, wvt_ref, qt_ref, k_ref, vt_ref):
    hq = _rms(cq_ref[...].astype(F32), gq_ref[...]).astype(BF16)
    hkv = _rms(ckv_ref[...].astype(F32), gkv_ref[...]).astype(BF16)
    kr = kr_ref[...].astype(F32)
    k_roped = (kr * cos_ref[...] + pltpu.roll(kr, QK_ROPE, 1) * sin_ref[...]).astype(BF16)
    cos_t = cos_t_ref[...]
    sin_t = sin_t_ref[...]
    nt = (((1,), (1,)), ((), ()))
    qt_all = lax.dot_general(wqt_ref[...], hq, nt, preferred_element_type=F32)
    kn_all = jnp.dot(hkv, wk_ref[...], preferred_element_type=F32)
    vt_all = lax.dot_general(wvt_ref[...], hkv, nt, preferred_element_type=F32)
    ones = jnp.ones((MLA_V_EXT - V_HEAD, hq.shape[0]), BF16)
    for h in range(MLA_HEADS):
        r0 = h * MLA_QK_PAD
        qt_ref[0, h, :QK_NOPE, :] = qt_all[r0:r0 + QK_NOPE].astype(BF16)
        y2 = qt_all[r0 + QK_NOPE:r0 + MLA_QK_PAD]
        partner = jnp.concatenate([y2[QK_ROPE:], y2[:QK_ROPE]], axis=0)
        qt_ref[0, h, QK_NOPE:, :] = (y2 * cos_t + partner * sin_t).astype(BF16)
        k_ref[0, h, :, :QK_NOPE] = kn_all[:, h * QK_NOPE:(h + 1) * QK_NOPE].astype(BF16)
        k_ref[0, h, :, QK_NOPE:] = k_roped
        vt_ref[0, h, :V_HEAD, :] = vt_all[h * V_HEAD:(h + 1) * V_HEAD, :].astype(BF16)
        vt_ref[0, h, V_HEAD:, :] = ones


def mla_projections(y, gq, gkv, cos, sin, wqt, wk, wvt, *, batch, seq, bm):
    nb = seq // bm
    cq_blk = MLA_WIDTH // Q_LORA
    kr_blk = (MLA_WIDTH + Q_LORA + KV_LORA) // LANES
    const2 = lambda b, i: (0, 0)
    return pl.pallas_call(
        _mla_proj_kernel,
        grid=(batch, nb),
        in_specs=[
            pl.BlockSpec((bm, Q_LORA), lambda b, i: (b * nb + i, cq_blk)),
            pl.BlockSpec((bm, KV_LORA), lambda b, i: (b * nb + i, cq_blk + 1)),
            pl.BlockSpec((bm, LANES), lambda b, i: (b * nb + i, kr_blk)),
            pl.BlockSpec((1, Q_LORA), const2),
            pl.BlockSpec((1, KV_LORA), const2),
            pl.BlockSpec((bm, LANES), lambda b, i: (i, 0)),
            pl.BlockSpec((bm, LANES), lambda b, i: (i, 0)),
            pl.BlockSpec((LANES, bm), lambda b, i: (0, i)),
            pl.BlockSpec((LANES, bm), lambda b, i: (0, i)),
            pl.BlockSpec(wqt.shape, const2),
            pl.BlockSpec(wk.shape, const2),
            pl.BlockSpec(wvt.shape, const2),
        ],
        out_specs=[
            pl.BlockSpec((1, MLA_HEADS, MLA_QK_PAD, bm), lambda b, i: (b, 0, 0, i)),
            pl.BlockSpec((1, MLA_HEADS, bm, MLA_QK_PAD), lambda b, i: (b, 0, i, 0)),
            pl.BlockSpec((1, MLA_HEADS, MLA_V_EXT, bm), lambda b, i: (b, 0, 0, i)),
        ],
        out_shape=[
            jax.ShapeDtypeStruct((batch, MLA_HEADS, MLA_QK_PAD, seq), BF16),
            jax.ShapeDtypeStruct((batch, MLA_HEADS, seq, MLA_QK_PAD), BF16),
            jax.ShapeDtypeStruct((batch, MLA_HEADS, MLA_V_EXT, seq), BF16),
        ],
        compiler_params=_params(("arbitrary", "arbitrary")),
        name="mla_projections",
    )(y, y, y, gq, gkv, cos, sin, cos.T, sin.T, wqt, wk, wvt)


def _flash_store(o_ref, acc):
    dv = o_ref.shape[2]
    o_t = acc[:dv] * (1.0 / acc[dv:dv + 1])
    o_ref[0] = o_t.T.astype(o_ref.dtype)


def _flash_kernel(qt_ref, k_ref, vt_ref, o_ref, m_ref, acc_ref, *, bkv, n_kv, first):
    qt = qt_ref[0, 0]
    seq = k_ref.shape[2]

    def scores(lo, hi):
        return jnp.dot(k_ref[0, 0, lo:hi, :], qt, preferred_element_type=F32)

    def weighted(lo, hi, pt):
        return jnp.dot(vt_ref[0, 0, :, lo:hi], pt.astype(BF16), preferred_element_type=F32)

    bounds = [(0, first)] + [(lo, min(lo + bkv, seq)) for lo in range(first, seq, bkv)]
    st = scores(*bounds[0])
    m0 = jnp.max(st, axis=0, keepdims=True)
    seen = m0
    acc = None
    for c, (lo, hi) in enumerate(bounds):
        st_next = scores(*bounds[c + 1]) if c + 1 < len(bounds) else None
        if c > 0:
            seen = jnp.maximum(seen, jnp.max(st, axis=0, keepdims=True))
        part = weighted(lo, hi, jnp.exp2(st - m0))
        acc = part if acc is None else acc + part
        st = st_next
    _flash_store(o_ref, acc)

    @pl.when(jnp.max(seen - m0) > FLASH_MAX_EXCESS)
    def _():
        m_ref[...] = jnp.full(m_ref.shape, -jnp.inf, F32)
        acc_ref[...] = jnp.zeros(acc_ref.shape, F32)

        def body(c, carry):
            start = pl.multiple_of(c * bkv, bkv)
            s = jnp.dot(k_ref[0, 0, pl.ds(start, bkv), :], qt, preferred_element_type=F32)
            m_old = m_ref[...]
            m_new = jnp.maximum(m_old, jnp.max(s, axis=0, keepdims=True))
            pv = jnp.dot(vt_ref[0, 0, :, pl.ds(start, bkv)], jnp.exp2(s - m_new).astype(BF16),
                         preferred_element_type=F32)
            acc_ref[...] = jnp.exp2(m_old - m_new) * acc_ref[...] + pv
            m_ref[...] = m_new
            return carry

        lax.fori_loop(0, n_kv, body, 0)
        _flash_store(o_ref, acc_ref[...])


def flash_attention(qt, k, vt, *, bq, bkv, first):
    batch, heads, dq, seq = qt.shape
    dv_ext = vt.shape[2]
    return pl.pallas_call(
        functools.partial(_flash_kernel, bkv=bkv, n_kv=seq // bkv, first=first),
        grid=(batch, heads, seq // bq),
        in_specs=[
            pl.BlockSpec((1, 1, dq, bq), lambda b, h, i: (b, h, 0, i)),
            pl.BlockSpec((1, 1, seq, dq), lambda b, h, i: (b, h, 0, 0)),
            pl.BlockSpec((1, 1, dv_ext, seq), lambda b, h, i: (b, h, 0, 0)),
        ],
        out_specs=pl.BlockSpec((1, bq, V_HEAD), lambda b, h, i: (b, i, h)),
        out_shape=jax.ShapeDtypeStruct((batch, seq, heads * V_HEAD), BF16),
        scratch_shapes=[pltpu.VMEM((1, bq), F32), pltpu.VMEM((dv_ext, bq), F32)],
        compiler_params=_params(("arbitrary", "arbitrary", "arbitrary")),
        name="flash_attention",
    )(qt, k, vt)


def _prep_mla_w_in(w):
    cq = w[:, :Q_LORA]
    ckv = w[:, Q_LORA:Q_LORA + KV_LORA]
    kr = w[:, Q_LORA + KV_LORA:Q_LORA + KV_LORA + QK_ROPE]
    z = w[:, Q_LORA + KV_LORA + QK_ROPE:]
    kr_e, kr_o = kr[:, 0::2], kr[:, 1::2]
    pad = jnp.zeros((w.shape[0], MLA_IN_PAD - MLA_WIDTH - Q_LORA - KV_LORA - 2 * QK_ROPE), w.dtype)
    return jnp.concatenate([z, cq, ckv, kr_e, kr_o, kr_o, kr_e, pad], axis=1).astype(BF16)


def _prep_mla_w_q(w):
    w = w.reshape(Q_LORA, MLA_HEADS, QK_HEAD) * (QK_HEAD ** -0.5 * LOG2E)
    nope, rope = w[..., :QK_NOPE], w[..., QK_NOPE:]
    r_e, r_o = rope[..., 0::2], rope[..., 1::2]
    ext = jnp.concatenate([nope, r_e, r_o, r_o, r_e], axis=-1)
    return ext.reshape(Q_LORA, MLA_HEADS * MLA_QK_PAD).astype(BF16)


def _prep_mla_w_kv(w):
    w = w.reshape(KV_LORA, MLA_HEADS, QK_NOPE + V_HEAD)
    wk = w[..., :QK_NOPE].reshape(KV_LORA, MLA_HEADS * QK_NOPE)
    wvt = w[..., QK_NOPE:].reshape(KV_LORA, MLA_HEADS * V_HEAD).T
    return wk.astype(BF16), wvt.astype(BF16)


def _rope_tables(length):
    inv_freq = 1.0 / (ROPE_THETA ** (jnp.arange(0, QK_ROPE, 2, dtype=F32) / QK_ROPE))
    ang = jnp.arange(length, dtype=F32)[:, None] * inv_freq[None, :]
    c, s = jnp.cos(ang), jnp.sin(ang)
    zeros = jnp.zeros((length, QK_ROPE), F32)
    return jnp.concatenate([c, c, zeros], axis=1), jnp.concatenate([-s, s, zeros], axis=1)


def kernel(x, norm_pre, norm_post, na_w_in, na_rpb, na_w_out, mla_w_in, mla_q_norm,
           mla_w_q_b, mla_kv_norm, mla_w_kv_b, mla_w_out):
    batch, seq, d = x.shape
    xt = x.reshape(batch * seq, d)
    cos_t, sin_t = _rope_tables(seq)
    t = TILES
    h = rms_norm(xt, norm_pre[0].reshape(1, d), bm=t["rms_rows"])
    for i in range(DEPTH):
        j = i // 2
        g_post = norm_post[i].reshape(1, d)
        g_next = norm_pre[i + 1].reshape(1, d) if i + 1 < DEPTH else None
        if i % 2 == 0:
            y = na_projection(h, na_w_in, j, sections=(3, 1), transposed=False,
                              scale_first=False, bm=t["proj_rows"], bn=t["proj_cols"])
            qvt = na_projection(h, na_w_in, j, sections=(0, 2), transposed=True,
                                scale_first=True, bm=t["proj_rows"], bn=t["proj_cols"])
            table = na_bias_table(na_rpb[j])
            o = na_attention(y, qvt, table, batch=batch, seq=seq, groups_per_step=t["na_groups"])
            w_out = na_w_out[j]
        else:
            y = matmul(h, _prep_mla_w_in(mla_w_in[j]), bm=t["mla_in_rows"], bn=MLA_IN_PAD // 2)
            wk, wvt = _prep_mla_w_kv(mla_w_kv_b[j])
            qt, k, vt = mla_projections(
                y, mla_q_norm[j].reshape(1, -1), mla_kv_norm[j].reshape(1, -1), cos_t, sin_t,
                _prep_mla_w_q(mla_w_q_b[j]).T, wk, wvt, batch=batch, seq=seq, bm=t["mla_proj_rows"])
            o = flash_attention(qt, k, vt, bq=t["flash_q"], bkv=t["flash_kv"], first=t["flash_first"])
            o = o.reshape(batch * seq, MLA_WIDTH)
            w_out = mla_w_out[j]
        xt, h = gate_out(o, y, w_out.astype(BF16), xt, g_post, g_next,
                         bm=t["gate_rows"], n_chains=t["gate_chains"])
    return xt.reshape(batch, seq, d)
```
